```python
import jax, jax.numpy as jnp
from jax import lax
import numpy as np

D_MODEL = 2048
BATCH = 8
SEQ = 4096
DEPTH = 4

DILATED_PAIRS = ((128, 1), (512, 4), (2048, 16))
N_A_GROUPS = len(DILATED_PAIRS)
A_HEAD_DIM = 128
A_WIDTH = D_MODEL // 4
A_HEADS_PER_GROUP = A_WIDTH // A_HEAD_DIM
A_HEADS = N_A_GROUPS * A_HEADS_PER_GROUP
GLA_HEADS = 4
GLA_DV_TOTAL = 3 * D_MODEL // 8
GLA_DK_TOTAL = GLA_DV_TOTAL // 2
GLA_DV = GLA_DV_TOTAL // GLA_HEADS
GLA_DK = GLA_DK_TOTAL // GLA_HEADS
GLA_GATE_RANK = 16
GLA_TAU = 16.0
GLA_CHUNK = 64
SGU_WIDTH = 3 * D_MODEL // 8
SGU_GROUPS = 4
SGU_CHUNK = 128
N_BRANCHES = 3
MIX_WIDTH = A_WIDTH + GLA_DV_TOTAL + SGU_WIDTH
D_FFN = ((8 * D_MODEL + 3 * 256 - 1) // (3 * 256)) * 256
REL_BUCKETS = 32
REL_MAX_DIST = 2048
EPS = 1e-6
IN_WIDTHS = (N_A_GROUPS * A_WIDTH, N_A_GROUPS * A_WIDTH, N_A_GROUPS * A_WIDTH,
             GLA_DK_TOTAL, GLA_DK_TOTAL, GLA_DV_TOTAL, GLA_GATE_RANK,
             SGU_WIDTH, SGU_WIDTH,
             D_MODEL, D_MODEL, D_MODEL)
D_IN = sum(IN_WIDTHS)

kernel_name = "hybrid_dilated_gla_sgu_gated_block"


def rms_norm(x, g):
    xf = x.astype(jnp.float32)
    y = xf * lax.rsqrt(jnp.mean(xf * xf, axis=-1, keepdims=True) + EPS)
    return (y * g.astype(jnp.float32)).astype(x.dtype)


def layer_norm(x, g, b):
    xf = x.astype(jnp.float32)
    mu = jnp.mean(xf, axis=-1, keepdims=True)
    xc = xf - mu
    y = xc * lax.rsqrt(jnp.mean(xc * xc, axis=-1, keepdims=True) + EPS)
    return (y * g.astype(jnp.float32) + b.astype(jnp.float32)).astype(x.dtype)


def t5_causal_bucket(dist):
    max_exact = REL_BUCKETS // 2
    d = np.maximum(dist, 1)
    large = max_exact + (np.log(d / max_exact) / np.log(REL_MAX_DIST / max_exact)
                         * (REL_BUCKETS - max_exact)).astype(np.int64)
    large = np.minimum(large, REL_BUCKETS - 1)
    return np.where(dist < max_exact, dist, large).astype(np.int32)


def dilated_window_attention(q, k, v, bias_heads, window, dilation):
    B, S, H, Dh = q.shape
    n_back = window // dilation
    blk = n_back
    L = S // dilation
    nb = -(-L // blk)
    Lp = nb * blk

    def to_sub(t):
        t = t.reshape(B, L, dilation, H, Dh).transpose(0, 2, 3, 1, 4)
        return jnp.pad(t, ((0, 0), (0, 0), (0, 0), (0, Lp - L), (0, 0)))

    def band(t):
        tp = jnp.pad(t, ((0, 0), (0, 0), (0, 0), (blk, 0), (0, 0))).reshape(B, dilation, H, nb + 1, blk, Dh)
        return jnp.concatenate([tp[:, :, :, :-1], tp[:, :, :, 1:]], axis=-2)

    qb = to_sub(q).reshape(B, dilation, H, nb, blk, Dh)
    kb = band(to_sub(k))
    vb = band(to_sub(v))

    i = np.arange(blk)[:, None]
    j = np.arange(2 * blk)[None, :]
    sub_dist = i + blk - j
    blk_id = np.arange(nb)[:, None, None]
    valid = (sub_dist >= 0) & (sub_dist <= n_back) & (blk_id * blk - blk + j >= 0)
    bucket = t5_causal_bucket(np.clip(sub_dist, 0, None) * dilation)
    bias = jnp.moveaxis(jnp.take(bias_heads, bucket, axis=0), -1, 0)

    logits = jnp.einsum('brhnqd,brhnkd->brhnqk', qb, kb) * (Dh ** -0.5) + bias[:, None]
    logits = jnp.where(valid, logits, -jnp.inf)
    m = jnp.max(logits, axis=-1, keepdims=True)
    p = jnp.exp(logits - m)
    s = jnp.sum(p, axis=-1, keepdims=True)
    o = jnp.einsum('brhnqk,brhnkd->brhnqd', p, vb) / s
    lse = (m + jnp.log(s))[..., 0]

    o = o.reshape(B, dilation, H, Lp, Dh)[:, :, :, :L].transpose(0, 3, 1, 2, 4).reshape(B, S, H, Dh)
    lse = lse.reshape(B, dilation, H, Lp)[..., :L].transpose(0, 3, 1, 2).reshape(B, S, H)
    return o, lse


def dilated_mixer(aq, ak, av, gq, gk, rel_bias):
    B, S, _ = aq.shape
    shp = (B, S, N_A_GROUPS, A_HEADS_PER_GROUP, A_HEAD_DIM)
    q = rms_norm(aq.reshape(shp), gq).astype(jnp.float32)
    k = rms_norm(ak.reshape(shp), gk).astype(jnp.float32)
    v = av.reshape(shp).astype(jnp.float32)
    outs, lses = [], []
    for gi, (window, dilation) in enumerate(DILATED_PAIRS):
        heads = slice(gi * A_HEADS_PER_GROUP, (gi + 1) * A_HEADS_PER_GROUP)
        o, lse = dilated_window_attention(q[:, :, gi], k[:, :, gi], v[:, :, gi],
                                          rel_bias[:, heads].astype(jnp.float32), window, dilation)
        outs.append(o)
        lses.append(lse)
    wts = jax.nn.softmax(jnp.stack(lses), axis=0)[..., None]
    o = jnp.sum(wts * jnp.stack(outs), axis=0)
    return o.reshape(B, S, A_WIDTH)


def gla_mixer(bq, bk, bv, b_gate_low, w_gate_up, b_gate, g_out):
    B, S, _ = bq.shape
    f32 = jnp.float32
    q = bq.reshape(B, S, GLA_HEADS, GLA_DK).astype(f32) * (GLA_DK ** -0.5)
    k = bk.reshape(B, S, GLA_HEADS, GLA_DK).astype(f32)
    v = bv.reshape(B, S, GLA_HEADS, GLA_DV).astype(f32)
    log_a = jax.nn.log_sigmoid((b_gate_low @ w_gate_up + b_gate).astype(f32)) / GLA_TAU
    log_a = log_a.reshape(B, S, GLA_HEADS, GLA_DK)
    n = S // GLA_CHUNK

    def chunks(t):
        return t.reshape(B, n, GLA_CHUNK, GLA_HEADS, t.shape[-1]).transpose(1, 0, 3, 2, 4)

    causal = np.tril(np.ones((GLA_CHUNK, GLA_CHUNK), dtype=bool))[:, :, None]

    def step(state, inp):
        qc, kc, vc, ac = inp
        b = jnp.cumsum(ac, axis=-2)
        inter = jnp.einsum('bhtk,bhkv->bhtv', qc * jnp.exp(b), state)
        diff = b[:, :, :, None, :] - b[:, :, None, :, :]
        decay = jnp.exp(jnp.where(causal, diff, -jnp.inf))
        scores = jnp.einsum('bhtk,bhsk,bhtsk->bhts', qc, kc, decay)
        intra = jnp.einsum('bhts,bhsv->bhtv', scores, vc)
        b_last = b[:, :, -1:, :]
        new_state = (jnp.exp(b_last[:, :, 0, :])[..., None] * state
                     + jnp.einsum('bhsk,bhsv->bhkv', kc * jnp.exp(b_last - b), vc))
        return new_state, inter + intra

    state0 = jnp.zeros((B, GLA_HEADS, GLA_DK, GLA_DV), f32)
    _, o = lax.scan(step, state0, (chunks(q), chunks(k), chunks(v), chunks(log_a)))
    o = o.transpose(1, 0, 3, 2, 4).reshape(B, S, GLA_HEADS, GLA_DV)
    o = rms_norm(o, g_out)
    return o.reshape(B, S, GLA_DV_TOTAL)


def sgu_mixer(cu, cv, ln_g, ln_b, w_s, b_s):
    B, S, _ = cu.shape
    u = jax.nn.gelu(cu)
    v = layer_norm(jax.nn.gelu(cv), ln_g, ln_b)
    n = S // SGU_CHUNK
    vg = v.reshape(B, n, SGU_CHUNK, SGU_GROUPS, SGU_WIDTH // SGU_GROUPS)
    mask = np.tril(np.ones((SGU_CHUNK, SGU_CHUNK), dtype=bool))
    w = jnp.where(mask, w_s, jnp.zeros_like(w_s))
    f = jnp.einsum('gts,bnsgc->bntgc', w, vg) + b_s.T[None, None, :, :, None]
    return u * f.reshape(B, S, SGU_WIDTH)


def setup_inputs(seed: int = 0) -> dict:
    key = jax.random.key(seed)
    ks = jax.random.split(key, 20)
    nrm = jax.random.normal
    f32 = jnp.float32
    x = nrm(ks[0], (BATCH, SEQ, D_MODEL), f32)
    rel_bias = 0.5 * nrm(ks[1], (REL_BUCKETS, A_HEADS), f32)
    norm1_g = 1.0 + 0.02 * nrm(ks[2], (DEPTH, D_MODEL), f32)
    w_in = nrm(ks[3], (DEPTH, D_MODEL, D_IN), f32) * D_MODEL ** -0.5
    q_norm_g = 1.0 + 0.02 * nrm(ks[4], (DEPTH, A_HEAD_DIM), f32)
    k_norm_g = 1.0 + 0.02 * nrm(ks[5], (DEPTH, A_HEAD_DIM), f32)
    gla_gate_up = nrm(ks[6], (DEPTH, GLA_GATE_RANK, GLA_DK_TOTAL), f32) * GLA_GATE_RANK ** -0.5
    gla_gate_b = 0.1 * nrm(ks[7], (DEPTH, GLA_DK_TOTAL), f32)
    gla_out_g = 1.0 + 0.02 * nrm(ks[8], (DEPTH, GLA_DV), f32)
    sgu_ln_g = 1.0 + 0.02 * nrm(ks[9], (DEPTH, SGU_WIDTH), f32)
    sgu_ln_b = 0.02 * nrm(ks[10], (DEPTH, SGU_WIDTH), f32)
    sgu_w = nrm(ks[11], (DEPTH, SGU_GROUPS, SGU_CHUNK, SGU_CHUNK), f32) * SGU_CHUNK ** -0.5
    sgu_b = 1.0 + 0.02 * nrm(ks[12], (DEPTH, SGU_GROUPS, SGU_CHUNK), f32)
    w_branch = jnp.concatenate([
        nrm(ks[13], (DEPTH, A_WIDTH, D_MODEL), f32) * A_WIDTH ** -0.5,
        nrm(ks[14], (DEPTH, GLA_DV_TOTAL, D_MODEL), f32) * GLA_DV_TOTAL ** -0.5,
        nrm(ks[15], (DEPTH, SGU_WIDTH, D_MODEL), f32) * SGU_WIDTH ** -0.5], axis=1)
    w_out = nrm(ks[16], (DEPTH, D_MODEL, D_MODEL), f32) * D_MODEL ** -0.5
    norm2_g = 1.0 + 0.02 * nrm(ks[17], (DEPTH, D_MODEL), f32)
    w_ffn_in = nrm(ks[18], (DEPTH, D_MODEL, 2 * D_FFN), f32) * D_MODEL ** -0.5
    w_ffn_out = nrm(ks[19], (DEPTH, D_FFN, D_MODEL), f32) * D_FFN ** -0.5
    return dict(x=x, rel_bias=rel_bias, norm1_g=norm1_g, w_in=w_in, q_norm_g=q_norm_g, k_norm_g=k_norm_g,
                gla_gate_up=gla_gate_up, gla_gate_b=gla_gate_b, gla_out_g=gla_out_g,
                sgu_ln_g=sgu_ln_g, sgu_ln_b=sgu_ln_b, sgu_w=sgu_w, sgu_b=sgu_b,
                w_branch=w_branch, w_out=w_out, norm2_g=norm2_g, w_ffn_in=w_ffn_in, w_ffn_out=w_ffn_out)


def reference(x, rel_bias, norm1_g, w_in, q_norm_g, k_norm_g, gla_gate_up, gla_gate_b, gla_out_g,
              sgu_ln_g, sgu_ln_b, sgu_w, sgu_b, w_branch, w_out, norm2_g, w_ffn_in, w_ffn_out):
    in_splits = [int(s) for s in np.cumsum(IN_WIDTHS)[:-1]]
    branch_splits = [A_WIDTH, A_WIDTH + GLA_DV_TOTAL]
    for l in range(DEPTH):
        h = rms_norm(x, norm1_g[l])
        (aq, ak, av, bq, bk, bv, b_gate_low, cu, cv,
         gate_a, gate_b, gate_c) = jnp.split(h @ w_in[l], in_splits, axis=-1)
        o_a = dilated_mixer(aq, ak, av, q_norm_g[l], k_norm_g[l], rel_bias).astype(x.dtype)
        o_b = gla_mixer(bq, bk, bv, b_gate_low, gla_gate_up[l], gla_gate_b[l], gla_out_g[l]).astype(x.dtype)
        o_c = sgu_mixer(cu, cv, sgu_ln_g[l], sgu_ln_b[l], sgu_w[l], sgu_b[l])
        p_a, p_b, p_c = jnp.split(w_branch[l], branch_splits, axis=0)
        y = (jax.nn.sigmoid(gate_a) * (o_a @ p_a)
             + jax.nn.sigmoid(gate_b) * (o_b @ p_b)
             + jax.nn.sigmoid(gate_c) * (o_c @ p_c))
        x = x + y @ w_out[l]
        h = rms_norm(x, norm2_g[l])
        f_gate, f_up = jnp.split(h @ w_ffn_in[l], 2, axis=-1)
        x = x + (jax.nn.silu(f_gate) * f_up) @ w_ffn_out[l]
    return x
```

```python
import functools

import numpy as np
import jax
import jax.numpy as jnp
from jax import lax
from jax.experimental import pallas as pl
from jax.experimental.pallas import tpu as pltpu

F32 = jnp.float32
BF16 = jnp.bfloat16

D_MODEL = 2048
DILATED_PAIRS = ((128, 1), (512, 4), (2048, 16))
N_A_GROUPS = 3
A_HEAD_DIM = 128
A_WIDTH = 512
A_HPG = 4
A_BLK = 128
GLA_HEADS = 4
GLA_DK = 96
GLA_DV = 192
GLA_DK_PAD = 128
GLA_DV_PAD = 256
GLA_RANK = 16
GLA_TAU = 16.0
GLA_CHUNK = 64
SGU_WIDTH = 768
SGU_GROUPS = 4
SGU_GW = SGU_WIDTH // SGU_GROUPS
SGU_CHUNK = 128
D_FFN = 5632
REL_BUCKETS = 32
REL_MAX_DIST = 2048
EPS = 1e-6
NEG = -1e30

_OFF_AQ, _OFF_AK, _OFF_AV = 0, 1536, 3072
_OFF_BQ, _OFF_BK, _OFF_BV = 4608, 4992, 5376
_OFF_GL = 6144
_OFF_CU, _OFF_CV = 6160, 6928
_OFF_GATES = 7696

VMEM_LIMIT_BYTES = 56 * 1024 * 1024

ATTN_TILE = 2048
GLA_TILE = 256


def _cparams(n_axes):
    return pltpu.CompilerParams(
        dimension_semantics=("arbitrary",) * n_axes,
        vmem_limit_bytes=VMEM_LIMIT_BYTES,
    )


def _rmsnorm_kernel(x_ref, g_ref, o_ref):
    x = x_ref[...]
    ms = jnp.mean(x * x, axis=-1, keepdims=True)
    o_ref[...] = (x * lax.rsqrt(ms + EPS) * g_ref[...]).astype(o_ref.dtype)


def _rmsnorm(x, g, tm=512):
    m, d = x.shape
    return pl.pallas_call(
        _rmsnorm_kernel,
        grid=(m // tm,),
        in_specs=[pl.BlockSpec((tm, d), lambda i: (i, 0)),
                  pl.BlockSpec((1, d), lambda i: (0, 0))],
        out_specs=pl.BlockSpec((tm, d), lambda i: (i, 0)),
        out_shape=jax.ShapeDtypeStruct((m, d), BF16),
        compiler_params=_cparams(1),
        name="rmsnorm",
    )(x, g.reshape(1, d))


def _attn_proj_kernel(h_ref, w_ref, gq_ref, gk_ref, o_ref):
    j = pl.program_id(1)
    acc = jnp.dot(h_ref[...], w_ref[...], preferred_element_type=F32)

    @pl.when(j < 2 * N_A_GROUPS)
    def _():
        g = jnp.where(j < N_A_GROUPS, gq_ref[...], gk_ref[...])
        for hh in range(A_HPG):
            a = acc[:, hh * A_HEAD_DIM:(hh + 1) * A_HEAD_DIM]
            ms = jnp.mean(a * a, axis=-1, keepdims=True)
            o_ref[hh] = (a * lax.rsqrt(ms + EPS) * g).astype(o_ref.dtype)

    @pl.when(j >= 2 * N_A_GROUPS)
    def _():
        for hh in range(A_HPG):
            o_ref[hh] = acc[:, hh * A_HEAD_DIM:(hh + 1) * A_HEAD_DIM].astype(o_ref.dtype)


def _attn_proj(h, w, gq, gk, tm=1024):
    m, d = h.shape
    n_tiles = w.shape[1] // A_WIDTH
    return pl.pallas_call(
        _attn_proj_kernel,
        grid=(m // tm, n_tiles),
        in_specs=[pl.BlockSpec((tm, d), lambda i, j: (i, 0)),
                  pl.BlockSpec((d, A_WIDTH), lambda i, j: (0, j)),
                  pl.BlockSpec((1, A_HEAD_DIM), lambda i, j: (0, 0)),
                  pl.BlockSpec((1, A_HEAD_DIM), lambda i, j: (0, 0))],
        out_specs=pl.BlockSpec((A_HPG, tm, A_HEAD_DIM), lambda i, j: (j, i, 0)),
        out_shape=jax.ShapeDtypeStruct((n_tiles * A_HPG, m, A_HEAD_DIM), BF16),
        compiler_params=_cparams(2),
        name="attn_proj",
    )(h, w, gq.reshape(1, -1), gk.reshape(1, -1))


def _gla_proj_kernel(h_ref, w_ref, o_ref, gl_ref):
    acc = jnp.dot(h_ref[...], w_ref[...], preferred_element_type=F32)
    n_blk = o_ref.shape[0]
    for c in range(n_blk):
        o_ref[c] = acc[:, c * 128:(c + 1) * 128].astype(o_ref.dtype)
    gl_ref[...] = acc[:, n_blk * 128:(n_blk + 1) * 128]


def _gla_proj(h, w, tm=512):
    m, d = h.shape
    n = w.shape[1]
    n_blk = n // 128 - 1
    return pl.pallas_call(
        _gla_proj_kernel,
        grid=(m // tm,),
        in_specs=[pl.BlockSpec((tm, d), lambda i: (i, 0)),
                  pl.BlockSpec((d, n), lambda i: (0, 0))],
        out_specs=[pl.BlockSpec((n_blk, tm, 128), lambda i: (0, i, 0)),
                   pl.BlockSpec((tm, 128), lambda i: (i, 0))],
        out_shape=[jax.ShapeDtypeStruct((n_blk, m, 128), BF16),
                   jax.ShapeDtypeStruct((m, 128), F32)],
        compiler_params=_cparams(1),
        name="gla_proj",
    )(h, w)


def _sgu_kernel(h_ref, w_ref, lng_ref, lnb_ref, ws_ref, bs_ref, o_ref):
    acc = jnp.dot(h_ref[...], w_ref[...], preferred_element_type=F32)
    u = jax.nn.gelu(acc[:, :SGU_WIDTH])
    gv = jax.nn.gelu(acc[:, SGU_WIDTH:])
    mu = jnp.mean(gv, axis=-1, keepdims=True)
    xc = gv - mu
    var = jnp.mean(xc * xc, axis=-1, keepdims=True)
    vh = (xc * lax.rsqrt(var + EPS) * lng_ref[...] + lnb_ref[...]).astype(BF16)
    half = 2 * SGU_GW
    lane = lax.broadcasted_iota(jnp.int32, (SGU_CHUNK, half), 1)
    first = lane < SGU_GW
    for c in range(h_ref.shape[0] // SGU_CHUNK):
        rows = slice(c * SGU_CHUNK, (c + 1) * SGU_CHUNK)
        for p in range(2):
            cols = slice(p * half, (p + 1) * half)
            vc = vh[rows, cols]
            r0 = jnp.dot(ws_ref[2 * p], vc, preferred_element_type=F32)
            r1 = jnp.dot(ws_ref[2 * p + 1], vc, preferred_element_type=F32)
            f = jnp.where(first, r0, r1) + bs_ref[:, cols]
            o_ref[rows, cols] = (u[rows, cols] * f).astype(o_ref.dtype)


def _sgu(h, w, ln_g, ln_b, ws, bs_tile, tm=512):
    m, d = h.shape
    n = w.shape[1]
    return pl.pallas_call(
        _sgu_kernel,
        grid=(m // tm,),
        in_specs=[pl.BlockSpec((tm, d), lambda i: (i, 0)),
                  pl.BlockSpec((d, n), lambda i: (0, 0)),
                  pl.BlockSpec((1, SGU_WIDTH), lambda i: (0, 0)),
                  pl.BlockSpec((1, SGU_WIDTH), lambda i: (0, 0)),
                  pl.BlockSpec((SGU_GROUPS, SGU_CHUNK, SGU_CHUNK), lambda i: (0, 0, 0)),
                  pl.BlockSpec((SGU_CHUNK, SGU_WIDTH), lambda i: (0, 0))],
        out_specs=pl.BlockSpec((tm, SGU_WIDTH), lambda i: (i, 0)),
        out_shape=jax.ShapeDtypeStruct((m, SGU_WIDTH), BF16),
        compiler_params=_cparams(1),
        name="sgu",
    )(h, w, ln_g.reshape(1, -1), ln_b.reshape(1, -1), ws, bs_tile)


def _gates_kernel(h_ref, w_ref, o_ref):
    acc = jnp.dot(h_ref[...], w_ref[...], preferred_element_type=F32)
    o_ref[...] = jax.nn.sigmoid(acc).astype(o_ref.dtype)


def _gates(h, w, tm=1024, tn=1024):
    m, d = h.shape
    n = w.shape[1]
    return pl.pallas_call(
        _gates_kernel,
        grid=(m // tm, n // tn),
        in_specs=[pl.BlockSpec((tm, d), lambda i, j: (i, 0)),
                  pl.BlockSpec((d, tn), lambda i, j: (0, j))],
        out_specs=pl.BlockSpec((tm, tn), lambda i, j: (i, j)),
        out_shape=jax.ShapeDtypeStruct((m, n), BF16),
        compiler_params=_cparams(2),
        name="gates",
    )(h, w)


def _t5_causal_bucket(dist):
    max_exact = REL_BUCKETS // 2
    d = np.maximum(dist, 1)
    large = max_exact + (np.log(d / max_exact) / np.log(REL_MAX_DIST / max_exact)
                         * (REL_BUCKETS - max_exact)).astype(np.int64)
    large = np.minimum(large, REL_BUCKETS - 1)
    return np.where(dist < max_exact, dist, large).astype(np.int32)


def _attn_bias_tiles(rel_bias):
    i = np.arange(A_BLK)[:, None]
    j = np.arange(2 * A_BLK)[None, :]
    sub_dist = i + A_BLK - j
    valid = (sub_dist >= 0) & (sub_dist <= A_BLK)
    tiles = []
    for gi, (_, dilation) in enumerate(DILATED_PAIRS):
        bucket = _t5_causal_bucket(np.clip(sub_dist, 0, None) * dilation)
        heads = rel_bias[:, gi * A_HPG:(gi + 1) * A_HPG].astype(F32)
        b = jnp.moveaxis(jnp.take(heads, bucket, axis=0), -1, 0)
        tiles.append(jnp.where(valid[None], b, NEG))
    return jnp.concatenate(tiles, axis=0)


def _attn_kernel(q_ref, kc_ref, vc_ref, kp_ref, vp_ref, bias_ref, o_ref, lse_ref,
                 qf, kf, vf, *, dilation):
    tile = pl.program_id(2)
    t_rows = q_ref.shape[0]
    p_rows = kp_ref.shape[0]
    n_blk = t_rows // (A_BLK * dilation)
    qf[...] = q_ref[...].astype(F32)
    kf[0:p_rows] = kp_ref[...].astype(F32)
    kf[p_rows:] = kc_ref[...].astype(F32)
    vf[0:p_rows] = vp_ref[...].astype(F32)
    vf[p_rows:] = vc_ref[...].astype(F32)
    bias = bias_ref[...]
    col = lax.broadcasted_iota(jnp.int32, (A_BLK, 2 * A_BLK), 1)
    no_prev = jnp.logical_and(tile == 0, col < A_BLK)
    scale = A_HEAD_DIM ** -0.5

    def rows(start, size):
        if dilation == 1:
            return pl.ds(start, size)
        return pl.ds(start, size, stride=dilation)

    for r in range(dilation):
        for n in range(n_blk):
            start = r + dilation * A_BLK * n
            q = qf[rows(start, A_BLK), :].astype(BF16)
            k = kf[rows(start, 2 * A_BLK), :].astype(BF16)
            v = vf[rows(start, 2 * A_BLK), :].astype(BF16)
            s = lax.dot_general(q, k, (((1,), (1,)), ((), ())), preferred_element_type=F32)
            s = s * scale + bias
            if n == 0:
                s = jnp.where(no_prev, NEG, s)
            mx = jnp.max(s, axis=-1, keepdims=True)
            p = jnp.exp(s - mx)
            l = jnp.sum(p, axis=-1, keepdims=True)
            o = jnp.dot(p.astype(BF16), v, preferred_element_type=F32) / l
            lse = mx + jnp.log(l)
            o_ref[rows(start, A_BLK), :] = o
            lse_ref[rows(start, A_BLK), :] = jnp.broadcast_to(lse, (A_BLK, A_HEAD_DIM))


def _attn_group(qkv, bias, gi, batch, seq):
    _, dilation = DILATED_PAIRS[gi]
    n_heads_total = N_A_GROUPS * A_HPG
    m = qkv.shape[1]
    tile = ATTN_TILE
    prev = A_BLK * dilation
    tiles_per_seq = seq // tile
    prev_per_tile = tile // prev
    prev_per_seq = seq // prev

    def cur_map(which):
        return lambda b, hh, t: (which * n_heads_total + gi * A_HPG + hh, b * tiles_per_seq + t, 0)

    def prev_map(which):
        def f(b, hh, t):
            blk = jnp.maximum(t * prev_per_tile - 1, 0)
            return (which * n_heads_total + gi * A_HPG + hh, b * prev_per_seq + blk, 0)
        return f

    out_map = lambda b, hh, t: (hh, b * tiles_per_seq + t, 0)
    return pl.pallas_call(
        functools.partial(_attn_kernel, dilation=dilation),
        grid=(batch, A_HPG, tiles_per_seq),
        in_specs=[pl.BlockSpec((None, tile, A_HEAD_DIM), cur_map(0)),
                  pl.BlockSpec((None, tile, A_HEAD_DIM), cur_map(1)),
                  pl.BlockSpec((None, tile, A_HEAD_DIM), cur_map(2)),
                  pl.BlockSpec((None, prev, A_HEAD_DIM), prev_map(1)),
                  pl.BlockSpec((None, prev, A_HEAD_DIM), prev_map(2)),
                  pl.BlockSpec((None, A_BLK, 2 * A_BLK), lambda b, hh, t: (gi * A_HPG + hh, 0, 0))],
        out_specs=[pl.BlockSpec((None, tile, A_HEAD_DIM), out_map),
                   pl.BlockSpec((None, tile, A_HEAD_DIM), out_map)],
        out_shape=[jax.ShapeDtypeStruct((A_HPG, m, A_HEAD_DIM), F32),
                   jax.ShapeDtypeStruct((A_HPG, m, A_HEAD_DIM), F32)],
        scratch_shapes=[pltpu.VMEM((tile, A_HEAD_DIM), F32),
                        pltpu.VMEM((tile + prev, A_HEAD_DIM), F32),
                        pltpu.VMEM((tile + prev, A_HEAD_DIM), F32)],
        compiler_params=_cparams(3),
        name=f"attn_g{gi}",
    )(qkv, qkv, qkv, qkv, qkv, bias)


def _gla_chunk_matrices(tile):
    t = np.arange(tile)
    same = (t[:, None] // GLA_CHUNK) == (t[None, :] // GLA_CHUNK)
    inc = same & (t[None, :] <= t[:, None])
    rev = same & (t[None, :] > t[:, None])
    return jnp.asarray(inc, F32), jnp.asarray(rev, F32)


def _gla_kernel(q_ref, k_ref, v_ref, gl_ref, wup_ref, bg_ref, gout_ref, inc_ref, rev_ref, o_ref, st_ref):
    tile = pl.program_id(2)

    @pl.when(tile == 0)
    def _():
        st_ref[...] = jnp.zeros_like(st_ref)

    hi = lax.Precision.HIGHEST
    x = jnp.dot(gl_ref[...], wup_ref[...], precision=hi, preferred_element_type=F32) + bg_ref[...]
    log_a = (jnp.minimum(x, 0.0) - jnp.log1p(jnp.exp(-jnp.abs(x)))) * (1.0 / GLA_TAU)
    b = jnp.dot(inc_ref[...], log_a, precision=hi, preferred_element_type=F32)
    b_rev = jnp.dot(rev_ref[...], log_a, precision=hi, preferred_element_type=F32)
    q = q_ref[...].astype(F32) * (GLA_DK ** -0.5)
    k = k_ref[...].astype(F32)
    v = jnp.concatenate([v_ref[0], v_ref[1]], axis=-1)
    qd = (q * jnp.exp(b)).astype(BF16)
    kd = (k * jnp.exp(-b)).astype(BF16)
    ks = (k * jnp.exp(b_rev)).astype(BF16)
    n_rows = q_ref.shape[0]
    causal = inc_ref[...] > 0.0
    scores = lax.dot_general(qd, kd, (((1,), (1,)), ((), ())), preferred_element_type=F32)
    scores = jnp.where(causal, scores, 0.0).astype(BF16)
    intra = jnp.dot(scores, v, preferred_element_type=F32)
    g_out = gout_ref[...]
    for c in range(n_rows // GLA_CHUNK):
        rows = slice(c * GLA_CHUNK, (c + 1) * GLA_CHUNK)
        st = st_ref[...]
        inter = lax.dot_general(qd[rows], st.astype(BF16), (((1,), (1,)), ((), ())),
                                preferred_element_type=F32)
        o = intra[rows] + inter
        ms = jnp.sum(o * o, axis=-1, keepdims=True) * (1.0 / GLA_DV)
        o_ref[rows, :] = (o * lax.rsqrt(ms + EPS) * g_out).astype(o_ref.dtype)
        last = c * GLA_CHUNK + GLA_CHUNK - 1
        decay = jnp.exp(b[last:last + 1, :])
        upd = lax.dot_general(v[rows], ks[rows], (((0,), (0,)), ((), ())), preferred_element_type=F32)
        st_ref[...] = st * decay + upd


def _gla(glaqkv, gl, wup, bg, gout, batch, seq):
    m = gl.shape[0]
    tile = GLA_TILE
    tiles_per_seq = seq // tile
    inc, rev = _gla_chunk_matrices(tile)
    row = lambda b, hh, t: b * tiles_per_seq + t
    return pl.pallas_call(
        _gla_kernel,
        grid=(batch, GLA_HEADS, tiles_per_seq),
        in_specs=[pl.BlockSpec((None, tile, 128), lambda b, hh, t: (hh, row(b, hh, t), 0)),
                  pl.BlockSpec((None, tile, 128), lambda b, hh, t: (GLA_HEADS + hh, row(b, hh, t), 0)),
                  pl.BlockSpec((2, tile, 128), lambda b, hh, t: (GLA_HEADS + hh, row(b, hh, t), 0)),
                  pl.BlockSpec((tile, 128), lambda b, hh, t: (row(b, hh, t), 0)),
                  pl.BlockSpec((None, 128, GLA_DK_PAD), lambda b, hh, t: (hh, 0, 0)),
                  pl.BlockSpec((None, 1, GLA_DK_PAD), lambda b, hh, t: (hh, 0, 0)),
                  pl.BlockSpec((1, GLA_DV_PAD), lambda b, hh, t: (0, 0)),
                  pl.BlockSpec((tile, tile), lambda b, hh, t: (0, 0)),
                  pl.BlockSpec((tile, tile), lambda b, hh, t: (0, 0))],
        out_specs=pl.BlockSpec((tile, GLA_DV_PAD), lambda b, hh, t: (row(b, hh, t), hh)),
        out_shape=jax.ShapeDtypeStruct((m, GLA_HEADS * GLA_DV_PAD), BF16),
        scratch_shapes=[pltpu.VMEM((GLA_DV_PAD, GLA_DK_PAD), F32)],
        compiler_params=_cparams(3),
        name="gla",
    )(glaqkv, glaqkv, glaqkv, gl, wup, bg, gout, inc, rev)


def _merge_kernel(o0_ref, o1_ref, o2_ref, l0_ref, l1_ref, l2_ref, ob_ref, oc_ref,
                  ga_ref, gb_ref, gc_ref, pa_ref, pb_ref, pc_ref, y_ref, oa_scr):
    j = pl.program_id(1)

    @pl.when(j == 0)
    def _():
        for hh in range(A_HPG):
            l0, l1, l2 = l0_ref[hh], l1_ref[hh], l2_ref[hh]
            mx = jnp.maximum(jnp.maximum(l0, l1), l2)
            w0, w1, w2 = jnp.exp(l0 - mx), jnp.exp(l1 - mx), jnp.exp(l2 - mx)
            o = (w0 * o0_ref[hh] + w1 * o1_ref[hh] + w2 * o2_ref[hh]) / (w0 + w1 + w2)
            oa_scr[:, hh * A_HEAD_DIM:(hh + 1) * A_HEAD_DIM] = o.astype(oa_scr.dtype)

    ya = jnp.dot(oa_scr[...], pa_ref[...], preferred_element_type=F32)
    yb = jnp.dot(ob_ref[...], pb_ref[...], preferred_element_type=F32)
    yc = jnp.dot(oc_ref[...], pc_ref[...], preferred_element_type=F32)
    y = (ga_ref[...].astype(F32) * ya + gb_ref[...].astype(F32) * yb + gc_ref[...].astype(F32) * yc)
    y_ref[...] = y.astype(y_ref.dtype)


def _merge(oas, lses, ob, oc, gates, pa, pb, pc, tm=512, tn=1024):
    m = ob.shape[0]
    n = pa.shape[1]
    gate_blocks = n // tn
    head_spec = pl.BlockSpec((A_HPG, tm, A_HEAD_DIM), lambda i, j: (0, i, 0))
    return pl.pallas_call(
        _merge_kernel,
        grid=(m // tm, n // tn),
        in_specs=[head_spec] * 6 + [
            pl.BlockSpec((tm, ob.shape[1]), lambda i, j: (i, 0)),
            pl.BlockSpec((tm, oc.shape[1]), lambda i, j: (i, 0)),
            pl.BlockSpec((tm, tn), lambda i, j: (i, j)),
            pl.BlockSpec((tm, tn), lambda i, j: (i, gate_blocks + j)),
            pl.BlockSpec((tm, tn), lambda i, j: (i, 2 * gate_blocks + j)),
            pl.BlockSpec((pa.shape[0], tn), lambda i, j: (0, j)),
            pl.BlockSpec((pb.shape[0], tn), lambda i, j: (0, j)),
            pl.BlockSpec((pc.shape[0], tn), lambda i, j: (0, j))],
        out_specs=pl.BlockSpec((tm, tn), lambda i, j: (i, j)),
        out_shape=jax.ShapeDtypeStruct((m, n), BF16),
        scratch_shapes=[pltpu.VMEM((tm, A_WIDTH), BF16)],
        compiler_params=_cparams(2),
        name="merge",
    )(*oas, *lses, ob, oc, gates, gates, gates, pa, pb, pc)


def _mm_res_kernel(a_ref, w_ref, x_ref, o_ref):
    k = pl.program_id(2)
    acc = jnp.dot(a_ref[...], w_ref[...], preferred_element_type=F32)

    @pl.when(k == 0)
    def _():
        o_ref[...] = x_ref[...] + acc

    @pl.when(k > 0)
    def _():
        o_ref[...] += acc


def _mm_res(a, w, x, tm=1024, tn=1024, tk=None):
    m, kdim = a.shape
    n = w.shape[1]
    tk = kdim if tk is None else tk
    return pl.pallas_call(
        _mm_res_kernel,
        grid=(m // tm, n // tn, kdim // tk),
        in_specs=[pl.BlockSpec((tm, tk), lambda i, j, k: (i, k)),
                  pl.BlockSpec((tk, tn), lambda i, j, k: (k, j)),
                  pl.BlockSpec((tm, tn), lambda i, j, k: (i, j))],
        out_specs=pl.BlockSpec((tm, tn), lambda i, j, k: (i, j)),
        out_shape=jax.ShapeDtypeStruct((m, n), F32),
        compiler_params=_cparams(3),
        name="mm_res",
    )(a, w, x)


def _ffn_in_kernel(h_ref, wg_ref, wu_ref, o_ref):
    h = h_ref[...]
    g = jnp.dot(h, wg_ref[...], preferred_element_type=F32)
    u = jnp.dot(h, wu_ref[...], preferred_element_type=F32)
    o_ref[...] = (g * jax.nn.sigmoid(g) * u).astype(o_ref.dtype)


def _ffn_in(h, w, tm=1024, tn=512):
    m, d = h.shape
    f = w.shape[1] // 2
    up_off = f // tn
    return pl.pallas_call(
        _ffn_in_kernel,
        grid=(m // tm, f // tn),
        in_specs=[pl.BlockSpec((tm, d), lambda i, j: (i, 0)),
                  pl.BlockSpec((d, tn), lambda i, j: (0, j)),
                  pl.BlockSpec((d, tn), lambda i, j: (0, up_off + j))],
        out_specs=pl.BlockSpec((tm, tn), lambda i, j: (i, j)),
        out_shape=jax.ShapeDtypeStruct((m, f), BF16),
        compiler_params=_cparams(2),
        name="ffn_in",
    )(h, w, w)


def _pad_heads(w, off, n_heads, width, padded):
    cols = w[:, off:off + n_heads * width].reshape(w.shape[0], n_heads, width)
    cols = jnp.pad(cols, ((0, 0), (0, 0), (0, padded - width)))
    return cols.reshape(w.shape[0], n_heads * padded)


def _layer_params(l, w_in, gla_gate_up, gla_gate_b, gla_out_g, sgu_w, sgu_b, w_branch, w_out, w_ffn_in, w_ffn_out):
    wl = w_in[l]
    w_attn = wl[:, _OFF_AQ:_OFF_BQ].astype(BF16)
    w_gla = jnp.concatenate([
        _pad_heads(wl, _OFF_BQ, GLA_HEADS, GLA_DK, GLA_DK_PAD),
        _pad_heads(wl, _OFF_BK, GLA_HEADS, GLA_DK, GLA_DK_PAD),
        _pad_heads(wl, _OFF_BV, GLA_HEADS, GLA_DV, GLA_DV_PAD),
        jnp.pad(wl[:, _OFF_GL:_OFF_GL + GLA_RANK], ((0, 0), (0, 128 - GLA_RANK)))], axis=1).astype(BF16)
    w_sgu = wl[:, _OFF_CU:_OFF_GATES].astype(BF16)
    w_gates = wl[:, _OFF_GATES:].astype(BF16)
    wup = gla_gate_up[l].reshape(GLA_RANK, GLA_HEADS, GLA_DK).transpose(1, 0, 2)
    wup = jnp.pad(wup, ((0, 0), (0, 128 - GLA_RANK), (0, GLA_DK_PAD - GLA_DK)))
    bg = jnp.pad(gla_gate_b[l].reshape(GLA_HEADS, 1, GLA_DK), ((0, 0), (0, 0), (0, GLA_DK_PAD - GLA_DK)))
    gout = jnp.pad(gla_out_g[l].reshape(1, GLA_DV), ((0, 0), (0, GLA_DV_PAD - GLA_DV)))
    tril = np.tril(np.ones((SGU_CHUNK, SGU_CHUNK), dtype=bool))
    ws = jnp.where(tril, sgu_w[l], 0.0).astype(BF16)
    bs_tile = jnp.repeat(sgu_b[l].T, SGU_GW, axis=1)
    wb = w_branch[l]
    pa = wb[:A_WIDTH].astype(BF16)
    pb = wb[A_WIDTH:A_WIDTH + GLA_HEADS * GLA_DV].reshape(GLA_HEADS, GLA_DV, D_MODEL)
    pb = jnp.pad(pb, ((0, 0), (0, GLA_DV_PAD - GLA_DV), (0, 0))).reshape(GLA_HEADS * GLA_DV_PAD, D_MODEL).astype(BF16)
    pc = wb[A_WIDTH + GLA_HEADS * GLA_DV:].astype(BF16)
    return dict(w_attn=w_attn, w_gla=w_gla, w_sgu=w_sgu, w_gates=w_gates, wup=wup, bg=bg, gout=gout,
                ws=ws, bs_tile=bs_tile, pa=pa, pb=pb, pc=pc,
                w_out=w_out[l].astype(BF16), w_ffn_in=w_ffn_in[l].astype(BF16),
                w_ffn_out=w_ffn_out[l].astype(BF16))


def kernel(x, rel_bias, norm1_g, w_in, q_norm_g, k_norm_g, gla_gate_up, gla_gate_b, gla_out_g,
           sgu_ln_g, sgu_ln_b, sgu_w, sgu_b, w_branch, w_out, norm2_g, w_ffn_in, w_ffn_out):
    batch, seq, d = x.shape
    depth = w_in.shape[0]
    m = batch * seq
    assert seq % ATTN_TILE == 0 and seq % GLA_TILE == 0 and d == D_MODEL
    xf = x.reshape(m, d)
    bias = _attn_bias_tiles(rel_bias)
    for l in range(depth):
        p = _layer_params(l, w_in, gla_gate_up, gla_gate_b, gla_out_g, sgu_w, sgu_b,
                          w_branch, w_out, w_ffn_in, w_ffn_out)
        h = _rmsnorm(xf, norm1_g[l])
        qkv = _attn_proj(h, p["w_attn"], q_norm_g[l], k_norm_g[l])
        glaqkv, gl = _gla_proj(h, p["w_gla"])
        o_c = _sgu(h, p["w_sgu"], sgu_ln_g[l], sgu_ln_b[l], p["ws"], p["bs_tile"])
        gates = _gates(h, p["w_gates"])
        oas, lses = [], []
        for gi in range(N_A_GROUPS):
            o_g, lse_g = _attn_group(qkv, bias, gi, batch, seq)
            oas.append(o_g)
            lses.append(lse_g)
        o_b = _gla(glaqkv, gl, p["wup"], p["bg"], p["gout"], batch, seq)
        y = _merge(oas, lses, o_b, o_c, gates, p["pa"], p["pb"], p["pc"])
        xf = _mm_res(y, p["w_out"], xf)
        h2 = _rmsnorm(xf, norm2_g[l])
        a = _ffn_in(h2, p["w_ffn_in"])
        xf = _mm_res(a, p["w_ffn_out"], xf, tk=D_FFN // 2)
    return xf.reshape(batch, seq, d)
```

```python
import functools

import numpy as np
import jax
import jax.numpy as jnp
from jax import lax
from jax.experimental import pallas as pl
from jax.experimental.pallas import tpu as pltpu

F32 = jnp.float32
BF16 = jnp.bfloat16

D_MODEL = 2048
DILATIONS = (1, 4, 16)
N_A_GROUPS = 3
A_HEAD_DIM = 128
A_WIDTH = 512
A_HPG = 4
A_BLK = 128
GLA_HEADS = 4
GLA_DK = 96
GLA_DV = 192
GLA_DK_PAD = 128
GLA_DV_PAD = 256
GLA_RANK = 16
GLA_TAU = 16.0
GLA_CHUNK = 64
GLA_LEVELS = (1, 2, 4, 8, 16, 32)
GLA_ATILE = 128
SGU_WIDTH = 768
SGU_GROUPS = 4
SGU_GW = SGU_WIDTH // SGU_GROUPS
SGU_CHUNK = 128
D_FFN = 5632
REL_BUCKETS = 32
REL_MAX_DIST = 2048
EPS = 1e-6
NEG = -1e30

_OFF_AQ, _OFF_AK, _OFF_AV = 0, 1536, 3072
_OFF_BQ, _OFF_BK, _OFF_BV = 4608, 4992, 5376
_OFF_GL = 6144
_OFF_CU, _OFF_CV = 6160, 6928
_OFF_GATES = 7696

VMEM_LIMIT_BYTES = 56 * 1024 * 1024

GLA_TILE = 256


def _cparams(n_axes):
    return pltpu.CompilerParams(
        dimension_semantics=("arbitrary",) * n_axes,
        vmem_limit_bytes=VMEM_LIMIT_BYTES,
    )


def _rms(x, g):
    ms = jnp.mean(x * x, axis=-1, keepdims=True)
    return x * lax.rsqrt(ms + EPS) * g


def _gates_kernel(x_ref, g_ref, w_ref, o_ref, h_ref):
    @pl.when(pl.program_id(1) == 0)
    def _():
        h_ref[...] = _rms(x_ref[...], g_ref[...]).astype(h_ref.dtype)

    acc = jnp.dot(h_ref[...], w_ref[...], preferred_element_type=F32)
    o_ref[...] = jax.nn.sigmoid(acc).astype(o_ref.dtype)


def _gates(x, g, w, l, tm=1024, tn=1024):
    m, d = x.shape
    n = w.shape[2]
    return pl.pallas_call(
        _gates_kernel,
        grid=(m // tm, n // tn),
        in_specs=[pl.BlockSpec((tm, d), lambda i, j: (i, 0)),
                  pl.BlockSpec((None, 1, d), lambda i, j: (l, 0, 0)),
                  pl.BlockSpec((None, d, tn), lambda i, j: (l, 0, j))],
        out_specs=[pl.BlockSpec((tm, tn), lambda i, j: (i, j)),
                   pl.BlockSpec((tm, d), lambda i, j: (i, 0))],
        out_shape=[jax.ShapeDtypeStruct((m, n), BF16),
                   jax.ShapeDtypeStruct((m, d), BF16)],
        compiler_params=_cparams(2),
        name="gates",
    )(x, g, w)


def _attn_proj_kernel(h_ref, w_ref, gq_ref, gk_ref, o_ref):
    acc = jnp.dot(h_ref[...], w_ref[...], preferred_element_type=F32)
    for which, g_ref in enumerate((gq_ref, gk_ref, None)):
        for hh in range(A_HPG):
            p = which * A_HPG + hh
            a = acc[:, p * A_HEAD_DIM:(p + 1) * A_HEAD_DIM]
            if g_ref is not None:
                a = _rms(a, g_ref[...])
            o_ref[p] = a.astype(o_ref.dtype)


def _attn_proj(h, w, gq, gk, l, gi, batch, seq):
    d_model = h.shape[1]
    dil = DILATIONS[gi]
    sub = seq // dil
    tm = min(sub, 1024)
    hv = h.reshape(batch, sub, dil * d_model)
    n = w.shape[3]
    planes = n // A_HEAD_DIM
    return pl.pallas_call(
        _attn_proj_kernel,
        grid=(batch, dil, sub // tm),
        in_specs=[pl.BlockSpec((None, tm, d_model), lambda b, r, t: (b, t, r)),
                  pl.BlockSpec((None, None, d_model, n), lambda b, r, t: (l, gi, 0, 0)),
                  pl.BlockSpec((None, 1, A_HEAD_DIM), lambda b, r, t: (l, 0, 0)),
                  pl.BlockSpec((None, 1, A_HEAD_DIM), lambda b, r, t: (l, 0, 0))],
        out_specs=pl.BlockSpec((planes, None, None, tm, A_HEAD_DIM), lambda b, r, t: (0, b, r, t, 0)),
        out_shape=jax.ShapeDtypeStruct((planes, batch, dil, sub, A_HEAD_DIM), BF16),
        compiler_params=_cparams(3),
        name=f"attn_proj_g{gi}",
    )(hv, w, gq, gk)


def _gla_proj_kernel(h_ref, w_ref, o_ref, gl_ref):
    acc = jnp.dot(h_ref[...], w_ref[...], preferred_element_type=F32)
    n_blk = o_ref.shape[0]
    for c in range(n_blk):
        o_ref[c] = acc[:, c * 128:(c + 1) * 128].astype(o_ref.dtype)
    gl_ref[...] = acc[:, n_blk * 128:(n_blk + 1) * 128]


def _gla_proj(h, w, l, tm=512):
    m, d = h.shape
    n = w.shape[2]
    n_blk = n // 128 - 1
    return pl.pallas_call(
        _gla_proj_kernel,
        grid=(m // tm,),
        in_specs=[pl.BlockSpec((tm, d), lambda i: (i, 0)),
                  pl.BlockSpec((None, d, n), lambda i: (l, 0, 0))],
        out_specs=[pl.BlockSpec((n_blk, tm, 128), lambda i: (0, i, 0)),
                   pl.BlockSpec((tm, 128), lambda i: (i, 0))],
        out_shape=[jax.ShapeDtypeStruct((n_blk, m, 128), BF16),
                   jax.ShapeDtypeStruct((m, 128), F32)],
        compiler_params=_cparams(1),
        name="gla_proj",
    )(h, w)


def _sgu_kernel(h_ref, w_ref, lng_ref, lnb_ref, ws_ref, bs_ref, o_ref):
    acc = jnp.dot(h_ref[...], w_ref[...], preferred_element_type=F32)
    u = jax.nn.gelu(acc[:, :SGU_WIDTH])
    gv = jax.nn.gelu(acc[:, SGU_WIDTH:])
    mu = jnp.mean(gv, axis=-1, keepdims=True)
    xc = gv - mu
    var = jnp.mean(xc * xc, axis=-1, keepdims=True)
    vh = (xc * lax.rsqrt(var + EPS) * lng_ref[...] + lnb_ref[...]).astype(BF16)
    half = 2 * SGU_GW
    lane = lax.broadcasted_iota(jnp.int32, (SGU_CHUNK, half), 1)
    first = lane < SGU_GW
    for c in range(h_ref.shape[0] // SGU_CHUNK):
        rows = slice(c * SGU_CHUNK, (c + 1) * SGU_CHUNK)
        for p in range(2):
            cols = slice(p * half, (p + 1) * half)
            vc = vh[rows, cols]
            r0 = jnp.dot(ws_ref[2 * p], vc, preferred_element_type=F32)
            r1 = jnp.dot(ws_ref[2 * p + 1], vc, preferred_element_type=F32)
            f = jnp.where(first, r0, r1) + bs_ref[:, cols]
            o_ref[rows, cols] = (u[rows, cols] * f).astype(o_ref.dtype)


def _sgu(h, w, ln_g, ln_b, ws, bs_tile, l, tm=512):
    m, d = h.shape
    n = w.shape[2]
    return pl.pallas_call(
        _sgu_kernel,
        grid=(m // tm,),
        in_specs=[pl.BlockSpec((tm, d), lambda i: (i, 0)),
                  pl.BlockSpec((None, d, n), lambda i: (l, 0, 0)),
                  pl.BlockSpec((None, 1, SGU_WIDTH), lambda i: (l, 0, 0)),
                  pl.BlockSpec((None, 1, SGU_WIDTH), lambda i: (l, 0, 0)),
                  pl.BlockSpec((None, SGU_GROUPS, SGU_CHUNK, SGU_CHUNK), lambda i: (l, 0, 0, 0)),
                  pl.BlockSpec((None, SGU_CHUNK, SGU_WIDTH), lambda i: (l, 0, 0))],
        out_specs=pl.BlockSpec((tm, SGU_WIDTH), lambda i: (i, 0)),
        out_shape=jax.ShapeDtypeStruct((m, SGU_WIDTH), BF16),
        compiler_params=_cparams(1),
        name="sgu",
    )(h, w, ln_g, ln_b, ws, bs_tile)


def _t5_causal_bucket(dist):
    max_exact = REL_BUCKETS // 2
    d = np.maximum(dist, 1)
    large = max_exact + (np.log(d / max_exact) / np.log(REL_MAX_DIST / max_exact)
                         * (REL_BUCKETS - max_exact)).astype(np.int64)
    large = np.minimum(large, REL_BUCKETS - 1)
    return np.where(dist < max_exact, dist, large).astype(np.int32)


def _attn_bias_tiles(rel_bias):
    i = np.arange(A_BLK)[:, None]
    j = np.arange(2 * A_BLK)[None, :]
    sub_dist = i + A_BLK - j
    valid = (sub_dist >= 0) & (sub_dist <= A_BLK)
    tiles = []
    for gi, dilation in enumerate(DILATIONS):
        bucket = _t5_causal_bucket(np.clip(sub_dist, 0, None) * dilation)
        heads = rel_bias[:, gi * A_HPG:(gi + 1) * A_HPG].astype(F32)
        b = jnp.moveaxis(jnp.take(heads, bucket, axis=0), -1, 0)
        tiles.append(jnp.where(valid[None], b, NEG))
    return jnp.stack(tiles, axis=0)


def _attn_unit(q, k, v, bias, scale):
    s = lax.dot_general(q, k, (((1,), (1,)), ((), ())), preferred_element_type=F32) * scale + bias
    mx = jnp.max(s, axis=-1, keepdims=True)
    p = jnp.exp(s - mx)
    den = jnp.sum(p, axis=-1, keepdims=True)
    o = jnp.dot(p.astype(BF16), v, preferred_element_type=F32) / den
    return o, mx + jnp.log(den)


def _attn_kernel(q0, k0, v0, q1, k1, v1, q2, k2, v2, bias_ref, o_ref, og, lg):
    scale = A_HEAD_DIM ** -0.5
    for gi, (q_ref, k_ref, v_ref) in enumerate(((q0, k0, v0), (q1, k1, v1), (q2, k2, v2))):
        dil = DILATIONS[gi]
        n_blk = q_ref.shape[1] // A_BLK
        bias = bias_ref[gi]
        for r in range(dil):
            for n in range(n_blk):
                q = q_ref[r, n * A_BLK:(n + 1) * A_BLK, :]
                if n == 0:
                    k = k_ref[r, 0:A_BLK, :]
                    v = v_ref[r, 0:A_BLK, :]
                    o, lse = _attn_unit(q, k, v, bias[:, A_BLK:], scale)
                else:
                    k = k_ref[r, (n - 1) * A_BLK:(n + 1) * A_BLK, :]
                    v = v_ref[r, (n - 1) * A_BLK:(n + 1) * A_BLK, :]
                    o, lse = _attn_unit(q, k, v, bias, scale)
                start = r + dil * A_BLK * n
                rows = pl.ds(start, A_BLK) if dil == 1 else pl.ds(start, A_BLK, stride=dil)
                og[gi, rows, :] = o
                lg[gi, rows, :] = jnp.broadcast_to(lse, (A_BLK, A_HEAD_DIM))
    l0, l1, l2 = lg[0], lg[1], lg[2]
    mx = jnp.maximum(jnp.maximum(l0, l1), l2)
    w0, w1, w2 = jnp.exp(l0 - mx), jnp.exp(l1 - mx), jnp.exp(l2 - mx)
    o = (w0 * og[0] + w1 * og[1] + w2 * og[2]) / (w0 + w1 + w2)
    o_ref[...] = o.astype(o_ref.dtype)


def _attention(qkvs, bias, batch, seq):
    in_specs = []
    args = []
    for gi, qkv in enumerate(qkvs):
        dil = DILATIONS[gi]
        for which in range(3):
            in_specs.append(pl.BlockSpec((None, None, dil, seq // dil, A_HEAD_DIM),
                                         lambda b, hh, which=which: (which * A_HPG + hh, b, 0, 0, 0)))
            args.append(qkv)
    in_specs.append(pl.BlockSpec((N_A_GROUPS, None, A_BLK, 2 * A_BLK), lambda b, hh: (0, hh, 0, 0)))
    return pl.pallas_call(
        _attn_kernel,
        grid=(batch, A_HPG),
        in_specs=in_specs,
        out_specs=pl.BlockSpec((None, seq, A_HEAD_DIM), lambda b, hh: (hh, b, 0)),
        out_shape=jax.ShapeDtypeStruct((A_HPG, batch * seq, A_HEAD_DIM), BF16),
        scratch_shapes=[pltpu.VMEM((N_A_GROUPS, seq, A_HEAD_DIM), F32),
                        pltpu.VMEM((N_A_GROUPS, seq, A_HEAD_DIM), F32)],
        compiler_params=_cparams(2),
        name="attention",
    )(*args, bias)


def _gla_static_matrices(tile):
    t = np.arange(tile)
    same = (t[:, None] // GLA_CHUNK) == (t[None, :] // GLA_CHUNK)
    inc = same & (t[None, :] <= t[:, None])
    a = np.arange(GLA_ATILE)
    ti, si = a[:, None], a[None, :]
    masks = []
    for m in GLA_LEVELS:
        masks.append((ti // (2 * m) == si // (2 * m)) & ((ti // m) % 2 == 1) & ((si // m) % 2 == 0))
    return jnp.asarray(inc, BF16), jnp.asarray(np.stack(masks), F32)


def _gla_kernel(q_ref, k_ref, v_ref, gl_ref, wh_ref, wl_ref, bg_ref, gout_ref, inc_ref, lvl_ref,
                o_ref, st_ref):
    @pl.when(pl.program_id(1) == 0)
    def _():
        st_ref[...] = jnp.zeros_like(st_ref)

    t_rows = gl_ref.shape[0]
    n_sub = t_rows // 8
    width = GLA_HEADS * GLA_DK_PAD

    def split(x):
        hi = x.astype(BF16)
        return hi, (x - hi.astype(F32)).astype(BF16)

    g_hi, g_lo = split(gl_ref[...])
    wh, wl = wh_ref[...], wl_ref[...]
    x = (jnp.dot(g_hi, wh, preferred_element_type=F32) + jnp.dot(g_hi, wl, preferred_element_type=F32)
         + jnp.dot(g_lo, wh, preferred_element_type=F32) + bg_ref[...])
    la = (jnp.minimum(x, 0.0) - jnp.log1p(jnp.exp(-jnp.abs(x)))) * (1.0 / GLA_TAU)
    la_hi, la_lo = split(la)
    inc = inc_ref[...]
    b = jnp.dot(inc, la_hi, preferred_element_type=F32) + jnp.dot(inc, la_lo, preferred_element_type=F32)

    b3 = b.reshape(n_sub, 8, width)
    la3 = la.reshape(n_sub, 8, width)
    sub = lax.broadcasted_iota(jnp.int32, (1, 8, width), 1)

    def hi_step(y, bit):
        return jnp.where((sub & bit) == 0, pltpu.roll(y, 8 - bit, axis=1), y)

    def lo_step(y, bit):
        return jnp.where((sub & bit) != 0, pltpu.roll(y, bit, axis=1), y)

    def spread(y, groups, pick):
        y4 = y.reshape(n_sub // groups, groups, 8, width)
        return jnp.broadcast_to(y4[:, pick:pick + 1], y4.shape).reshape(n_sub, 8, width)

    hi = {1: b3}
    lo = {1: b3 - la3}
    for bit in (1, 2, 4):
        hi[2 * bit] = hi_step(hi[bit], bit)
        lo[2 * bit] = lo_step(lo[bit], bit)
    for groups in (2, 4, 8):
        hi[8 * groups] = spread(hi[8], groups, groups - 1)
        lo[8 * groups] = spread(lo[8], groups, 0)

    q_all = jnp.concatenate([q_ref[h] for h in range(GLA_HEADS)], axis=1).astype(F32) * (GLA_DK ** -0.5)
    k_all = jnp.concatenate([k_ref[h] for h in range(GLA_HEADS)], axis=1).astype(F32)
    q3 = q_all.reshape(n_sub, 8, width)
    k3 = k_all.reshape(n_sub, 8, width)

    def q_side(m):
        return (q3 * jnp.exp(b3 - lo[m])).reshape(t_rows, width).astype(BF16)

    def k_side(m):
        return (k3 * jnp.exp(hi[m] - b3)).reshape(t_rows, width).astype(BF16)

    qd = {m: q_side(m) for m in GLA_LEVELS + (GLA_CHUNK,)}
    kd = {m: (k_all.astype(BF16) if m == 1 else k_side(m)) for m in GLA_LEVELS + (GLA_CHUNK,)}
    diag = q_all * k_all
    v_heads = [jnp.concatenate([v_ref[2 * h], v_ref[2 * h + 1]], axis=1) for h in range(GLA_HEADS)]
    g_out = gout_ref[...]
    nt = (((1,), (1,)), ((), ()))
    chunks_per_atile = GLA_ATILE // GLA_CHUNK

    for a in range(t_rows // GLA_ATILE):
        arows = slice(a * GLA_ATILE, (a + 1) * GLA_ATILE)
        for h in range(GLA_HEADS):
            cols = slice(h * GLA_DK_PAD, (h + 1) * GLA_DK_PAD)
            scores = None
            for li, m in enumerate(GLA_LEVELS):
                s = lax.dot_general(qd[m][arows, cols], kd[m][arows, cols], nt, preferred_element_type=F32)
                s = s * lvl_ref[li]
                scores = s if scores is None else scores + s
            v_a = v_heads[h][arows]
            rd = jnp.sum(diag[arows, cols], axis=-1, keepdims=True)
            intra = jnp.dot(scores.astype(BF16), v_a, preferred_element_type=F32) + rd * v_a.astype(F32)
            for cc in range(chunks_per_atile):
                c = a * chunks_per_atile + cc
                rows = slice(c * GLA_CHUNK, (c + 1) * GLA_CHUNK)
                st = st_ref[h]
                inter = lax.dot_general(qd[GLA_CHUNK][rows, cols], st.astype(BF16), nt,
                                        preferred_element_type=F32)
                o = intra[cc * GLA_CHUNK:(cc + 1) * GLA_CHUNK] + inter
                ms = jnp.sum(o * o, axis=-1, keepdims=True) * (1.0 / GLA_DV)
                o_ref[rows, h * GLA_DV_PAD:(h + 1) * GLA_DV_PAD] = (
                    o * lax.rsqrt(ms + EPS) * g_out).astype(o_ref.dtype)
                last = c * GLA_CHUNK + GLA_CHUNK - 1
                decay = jnp.exp(b[last:last + 1, cols])
                upd = lax.dot_general(v_heads[h][rows], kd[GLA_CHUNK][rows, cols],
                                      (((0,), (0,)), ((), ())), preferred_element_type=F32)
                st_ref[h] = st * decay + upd


def _gla(glaqkv, gl, wup_hi, wup_lo, bg, gout, l, batch, seq):
    m = gl.shape[0]
    tile = GLA_TILE
    tiles_per_seq = seq // tile
    inc, lvl = _gla_static_matrices(tile)
    row = lambda b, t: b * tiles_per_seq + t
    width = GLA_HEADS * GLA_DK_PAD
    return pl.pallas_call(
        _gla_kernel,
        grid=(batch, tiles_per_seq),
        in_specs=[pl.BlockSpec((GLA_HEADS, tile, 128), lambda b, t: (0, row(b, t), 0)),
                  pl.BlockSpec((GLA_HEADS, tile, 128), lambda b, t: (1, row(b, t), 0)),
                  pl.BlockSpec((2 * GLA_HEADS, tile, 128), lambda b, t: (1, row(b, t), 0)),
                  pl.BlockSpec((tile, 128), lambda b, t: (row(b, t), 0)),
                  pl.BlockSpec((None, 128, width), lambda b, t: (l, 0, 0)),
                  pl.BlockSpec((None, 128, width), lambda b, t: (l, 0, 0)),
                  pl.BlockSpec((None, 1, width), lambda b, t: (l, 0, 0)),
                  pl.BlockSpec((None, 1, GLA_DV_PAD), lambda b, t: (l, 0, 0)),
                  pl.BlockSpec((tile, tile), lambda b, t: (0, 0)),
                  pl.BlockSpec((len(GLA_LEVELS), GLA_ATILE, GLA_ATILE), lambda b, t: (0, 0, 0))],
        out_specs=pl.BlockSpec((tile, GLA_HEADS * GLA_DV_PAD), lambda b, t: (row(b, t), 0)),
        out_shape=jax.ShapeDtypeStruct((m, GLA_HEADS * GLA_DV_PAD), BF16),
        scratch_shapes=[pltpu.VMEM((GLA_HEADS, GLA_DV_PAD, GLA_DK_PAD), F32)],
        compiler_params=_cparams(2),
        name="gla",
    )(glaqkv, glaqkv, glaqkv, gl, wup_hi, wup_lo, bg, gout, inc, lvl)


def _merge_kernel(oa_ref, ob_ref, oc_ref, ga_ref, gb_ref, gc_ref, pa_ref, pb_ref, pc_ref, y_ref):
    oa = jnp.concatenate([oa_ref[hh] for hh in range(A_HPG)], axis=1)
    ya = jnp.dot(oa, pa_ref[...], preferred_element_type=F32)
    yb = jnp.dot(ob_ref[...], pb_ref[...], preferred_element_type=F32)
    yc = jnp.dot(oc_ref[...], pc_ref[...], preferred_element_type=F32)
    y = (ga_ref[...].astype(F32) * ya + gb_ref[...].astype(F32) * yb + gc_ref[...].astype(F32) * yc)
    y_ref[...] = y.astype(y_ref.dtype)


def _merge(oa, ob, oc, gates, pa, pb, pc, l, tm=256):
    m = ob.shape[0]
    n = pa.shape[2]
    const = lambda i: (l, 0, 0)
    return pl.pallas_call(
        _merge_kernel,
        grid=(m // tm,),
        in_specs=[pl.BlockSpec((A_HPG, tm, A_HEAD_DIM), lambda i: (0, i, 0)),
                  pl.BlockSpec((tm, ob.shape[1]), lambda i: (i, 0)),
                  pl.BlockSpec((tm, oc.shape[1]), lambda i: (i, 0)),
                  pl.BlockSpec((tm, n), lambda i: (i, 0)),
                  pl.BlockSpec((tm, n), lambda i: (i, 1)),
                  pl.BlockSpec((tm, n), lambda i: (i, 2)),
                  pl.BlockSpec((None,) + pa.shape[1:], const),
                  pl.BlockSpec((None,) + pb.shape[1:], const),
                  pl.BlockSpec((None,) + pc.shape[1:], const)],
        out_specs=pl.BlockSpec((tm, n), lambda i: (i, 0)),
        out_shape=jax.ShapeDtypeStruct((m, n), BF16),
        compiler_params=_cparams(1),
        name="merge",
    )(oa, ob, oc, gates, gates, gates, pa, pb, pc)


def _mm_res_kernel(a_ref, w_ref, x_ref, o_ref):
    k = pl.program_id(2)
    acc = jnp.dot(a_ref[...], w_ref[...], preferred_element_type=F32)

    @pl.when(k == 0)
    def _():
        o_ref[...] = x_ref[...] + acc

    @pl.when(k > 0)
    def _():
        o_ref[...] += acc


def _mm_res(a, w, x, l, tm=1024, tn=1024, tk=None):
    m, kdim = a.shape
    n = w.shape[2]
    tk = kdim if tk is None else tk
    return pl.pallas_call(
        _mm_res_kernel,
        grid=(m // tm, n // tn, kdim // tk),
        in_specs=[pl.BlockSpec((tm, tk), lambda i, j, k: (i, k)),
                  pl.BlockSpec((None, tk, tn), lambda i, j, k: (l, k, j)),
                  pl.BlockSpec((tm, tn), lambda i, j, k: (i, j))],
        out_specs=pl.BlockSpec((tm, tn), lambda i, j, k: (i, j)),
        out_shape=jax.ShapeDtypeStruct((m, n), F32),
        compiler_params=_cparams(3),
        name="mm_res",
    )(a, w, x)


def _ffn_in_kernel(x_ref, g_ref, wg_ref, wu_ref, o_ref, h_scr):
    @pl.when(pl.program_id(1) == 0)
    def _():
        h_scr[...] = _rms(x_ref[...], g_ref[...]).astype(h_scr.dtype)

    h = h_scr[...]
    g = jnp.dot(h, wg_ref[...], preferred_element_type=F32)
    u = jnp.dot(h, wu_ref[...], preferred_element_type=F32)
    o_ref[...] = (g * jax.nn.sigmoid(g) * u).astype(o_ref.dtype)


def _ffn_in(x, g, w, l, tm=1024, tn=512):
    m, d = x.shape
    f = w.shape[2] // 2
    up_off = f // tn
    return pl.pallas_call(
        _ffn_in_kernel,
        grid=(m // tm, f // tn),
        in_specs=[pl.BlockSpec((tm, d), lambda i, j: (i, 0)),
                  pl.BlockSpec((None, 1, d), lambda i, j: (l, 0, 0)),
                  pl.BlockSpec((None, d, tn), lambda i, j: (l, 0, j)),
                  pl.BlockSpec((None, d, tn), lambda i, j: (l, 0, up_off + j))],
        out_specs=pl.BlockSpec((tm, tn), lambda i, j: (i, j)),
        out_shape=jax.ShapeDtypeStruct((m, f), BF16),
        scratch_shapes=[pltpu.VMEM((tm, d), BF16)],
        compiler_params=_cparams(2),
        name="ffn_in",
    )(x, g, w, w)


def _pad_heads(w, off, n_heads, width, padded):
    lead = w.shape[:-1]
    cols = w[..., off:off + n_heads * width].reshape(lead + (n_heads, width))
    cols = jnp.pad(cols, [(0, 0)] * (len(lead) + 1) + [(0, padded - width)])
    return cols.reshape(lead + (n_heads * padded,))


def _prepare_params(w_in, gla_gate_up, gla_gate_b, gla_out_g, sgu_w, sgu_b, w_branch):
    depth = w_in.shape[0]
    w_attn = jnp.stack([
        jnp.concatenate([w_in[:, :, off + gi * A_WIDTH: off + (gi + 1) * A_WIDTH]
                         for off in (_OFF_AQ, _OFF_AK, _OFF_AV)], axis=-1)
        for gi in range(N_A_GROUPS)], axis=1).astype(BF16)
    w_gla = jnp.concatenate([
        _pad_heads(w_in, _OFF_BQ, GLA_HEADS, GLA_DK, GLA_DK_PAD),
        _pad_heads(w_in, _OFF_BK, GLA_HEADS, GLA_DK, GLA_DK_PAD),
        _pad_heads(w_in, _OFF_BV, GLA_HEADS, GLA_DV, GLA_DV_PAD),
        jnp.pad(w_in[:, :, _OFF_GL:_OFF_GL + GLA_RANK], ((0, 0), (0, 0), (0, 128 - GLA_RANK)))],
        axis=-1).astype(BF16)
    w_sgu = w_in[:, :, _OFF_CU:_OFF_GATES].astype(BF16)
    w_gates = w_in[:, :, _OFF_GATES:].astype(BF16)
    wup = _pad_heads(gla_gate_up, 0, GLA_HEADS, GLA_DK, GLA_DK_PAD)
    wup = jnp.pad(wup, ((0, 0), (0, 128 - GLA_RANK), (0, 0)))
    wup_hi = wup.astype(BF16)
    wup_lo = (wup - wup_hi.astype(F32)).astype(BF16)
    bg = _pad_heads(gla_gate_b.reshape(depth, 1, -1), 0, GLA_HEADS, GLA_DK, GLA_DK_PAD)
    gout = jnp.pad(gla_out_g.reshape(depth, 1, GLA_DV), ((0, 0), (0, 0), (0, GLA_DV_PAD - GLA_DV)))
    tril = np.tril(np.ones((SGU_CHUNK, SGU_CHUNK), dtype=bool))
    ws = jnp.where(tril, sgu_w, 0.0).astype(BF16)
    bs_tile = jnp.repeat(jnp.swapaxes(sgu_b, 1, 2), SGU_GW, axis=2)
    pa = w_branch[:, :A_WIDTH].astype(BF16)
    pb = w_branch[:, A_WIDTH:A_WIDTH + GLA_HEADS * GLA_DV].reshape(depth, GLA_HEADS, GLA_DV, D_MODEL)
    pb = jnp.pad(pb, ((0, 0), (0, 0), (0, GLA_DV_PAD - GLA_DV), (0, 0)))
    pb = pb.reshape(depth, GLA_HEADS * GLA_DV_PAD, D_MODEL).astype(BF16)
    pc = w_branch[:, A_WIDTH + GLA_HEADS * GLA_DV:].astype(BF16)
    return dict(w_attn=w_attn, w_gla=w_gla, w_sgu=w_sgu, w_gates=w_gates, wup_hi=wup_hi, wup_lo=wup_lo,
                bg=bg, gout=gout, ws=ws, bs_tile=bs_tile, pa=pa, pb=pb, pc=pc)


def kernel(x, rel_bias, norm1_g, w_in, q_norm_g, k_norm_g, gla_gate_up, gla_gate_b, gla_out_g,
           sgu_ln_g, sgu_ln_b, sgu_w, sgu_b, w_branch, w_out, norm2_g, w_ffn_in, w_ffn_out):
    batch, seq, d = x.shape
    depth = w_in.shape[0]
    m = batch * seq
    assert seq % (A_BLK * max(DILATIONS)) == 0 and seq % GLA_TILE == 0 and d == D_MODEL
    p = _prepare_params(w_in, gla_gate_up, gla_gate_b, gla_out_g, sgu_w, sgu_b, w_branch)
    w_out_b = w_out.astype(BF16)
    w_ffn_in_b = w_ffn_in.astype(BF16)
    w_ffn_out_b = w_ffn_out.astype(BF16)
    n1 = norm1_g.reshape(depth, 1, d)
    n2 = norm2_g.reshape(depth, 1, d)
    gq = q_norm_g.reshape(depth, 1, A_HEAD_DIM)
    gk = k_norm_g.reshape(depth, 1, A_HEAD_DIM)
    ln_g = sgu_ln_g.reshape(depth, 1, SGU_WIDTH)
    ln_b = sgu_ln_b.reshape(depth, 1, SGU_WIDTH)
    bias = _attn_bias_tiles(rel_bias)
    xf = x.reshape(m, d)
    for l in range(depth):
        gates, h = _gates(xf, n1, p["w_gates"], l)
        qkvs = [_attn_proj(h, p["w_attn"], gq, gk, l, gi, batch, seq) for gi in range(N_A_GROUPS)]
        glaqkv, gl = _gla_proj(h, p["w_gla"], l)
        o_c = _sgu(h, p["w_sgu"], ln_g, ln_b, p["ws"], p["bs_tile"], l)
        o_a = _attention(qkvs, bias, batch, seq)
        o_b = _gla(glaqkv, gl, p["wup_hi"], p["wup_lo"], p["bg"], p["gout"], l, batch, seq)
        y = _merge(o_a, o_b, o_c, gates, p["pa"], p["pb"], p["pc"], l)
        xf = _mm_res(y, w_out_b, xf, l)
        a = _ffn_in(xf, n2, w_ffn_in_b, l)
        xf = _mm_res(a, w_ffn_out_b, xf, l, tk=D_FFN // 2)
    return xf.reshape(batch, seq, d)
```

```python
import functools

import numpy as np
import jax
import jax.numpy as jnp
from jax import lax
from jax.experimental import pallas as pl
from jax.experimental.pallas import tpu as pltpu

F32 = jnp.float32
BF16 = jnp.bfloat16

D_MODEL = 2048
DILATIONS = (1, 4, 16)
N_A_GROUPS = 3
A_HEAD_DIM = 128
A_WIDTH = 512
A_HPG = 4
A_BLK = 128
GLA_HEADS = 4
GLA_DK = 96
GLA_DV = 192
GLA_DK_PAD = 128
GLA_DV_PAD = 256
GLA_RANK = 16
GLA_TAU = 16.0
GLA_CHUNK = 64
GLA_LEVELS = (1, 2, 4, 8, 16, 32)
GLA_ATILE = 128
SGU_WIDTH = 768
SGU_GROUPS = 4
SGU_GW = SGU_WIDTH // SGU_GROUPS
SGU_CHUNK = 128
D_FFN = 5632
REL_BUCKETS = 32
REL_MAX_DIST = 2048
EPS = 1e-6
NEG = -1e30

_OFF_AQ, _OFF_AK, _OFF_AV = 0, 1536, 3072
_OFF_BQ, _OFF_BK, _OFF_BV = 4608, 4992, 5376
_OFF_GL = 6144
_OFF_CU, _OFF_CV = 6160, 6928
_OFF_GATES = 7696

VMEM_LIMIT_BYTES = 56 * 1024 * 1024

GLA_TILE = 256
NORM_CHUNK = 256


def _cparams(n_axes):
    return pltpu.CompilerParams(
        dimension_semantics=("arbitrary",) * n_axes,
        vmem_limit_bytes=VMEM_LIMIT_BYTES,
    )


def _rms(x, g):
    ms = jnp.mean(x * x, axis=-1, keepdims=True)
    return x * lax.rsqrt(ms + EPS) * g


def _gates_kernel(x_ref, g_ref, w_ref, o_ref, h_ref):
    def tile(h):
        acc = jnp.dot(h, w_ref[...], preferred_element_type=F32)
        return jax.nn.sigmoid(acc).astype(o_ref.dtype)

    @pl.when(pl.program_id(1) == 0)
    def _():
        for c in range(x_ref.shape[0] // NORM_CHUNK):
            rows = slice(c * NORM_CHUNK, (c + 1) * NORM_CHUNK)
            h = _rms(x_ref[rows, :], g_ref[...]).astype(h_ref.dtype)
            h_ref[rows, :] = h
            o_ref[rows, :] = tile(h)

    @pl.when(pl.program_id(1) > 0)
    def _():
        o_ref[...] = tile(h_ref[...])


def _gates(x, g, w, l, tm=1024, tn=1024):
    m, d = x.shape
    n = w.shape[2]
    return pl.pallas_call(
        _gates_kernel,
        grid=(m // tm, n // tn),
        in_specs=[pl.BlockSpec((tm, d), lambda i, j: (i, 0)),
                  pl.BlockSpec((None, 1, d), lambda i, j: (l, 0, 0)),
                  pl.BlockSpec((None, d, tn), lambda i, j: (l, 0, j))],
        out_specs=[pl.BlockSpec((tm, tn), lambda i, j: (i, j)),
                   pl.BlockSpec((tm, d), lambda i, j: (i, 0))],
        out_shape=[jax.ShapeDtypeStruct((m, n), BF16),
                   jax.ShapeDtypeStruct((m, d), BF16)],
        compiler_params=_cparams(2),
        name="gates",
    )(x, g, w)


def _attn_proj_kernel(h_ref, wq_ref, wk_ref, wv_ref, gq_ref, gk_ref, o_ref, *scratch, dil):
    h = h_ref[...]
    sub_rows = h.shape[0] // dil
    for which, (w_ref, g_ref) in enumerate(((wq_ref, gq_ref), (wk_ref, gk_ref), (wv_ref, None))):
        acc = jnp.dot(h, w_ref[...], preferred_element_type=F32)
        for hh in range(A_HPG):
            p = which * A_HPG + hh
            a = acc[:, hh * A_HEAD_DIM:(hh + 1) * A_HEAD_DIM]
            if g_ref is not None:
                a = _rms(a, g_ref[...])
            if dil == 1:
                o_ref[p, 0] = a.astype(o_ref.dtype)
            else:
                scr = scratch[0]
                scr[p] = a
                for r in range(dil):
                    o_ref[p, r] = scr[p, pl.ds(r, sub_rows, stride=dil), :].astype(o_ref.dtype)


def _attn_proj(h, w, gq, gk, l, gi, batch, seq, tm=1024):
    d_model = h.shape[1]
    dil = DILATIONS[gi]
    planes = 3 * A_HPG
    tiles_per_seq = seq // tm
    w_spec = lambda which: pl.BlockSpec((None, d_model, A_WIDTH),
                                        lambda b, t: (l, 0, which * N_A_GROUPS + gi))
    return pl.pallas_call(
        functools.partial(_attn_proj_kernel, dil=dil),
        grid=(batch, tiles_per_seq),
        in_specs=[pl.BlockSpec((tm, d_model), lambda b, t: (b * tiles_per_seq + t, 0)),
                  w_spec(0), w_spec(1), w_spec(2),
                  pl.BlockSpec((None, 1, A_HEAD_DIM), lambda b, t: (l, 0, 0)),
                  pl.BlockSpec((None, 1, A_HEAD_DIM), lambda b, t: (l, 0, 0))],
        out_specs=pl.BlockSpec((planes, None, dil, tm // dil, A_HEAD_DIM), lambda b, t: (0, b, 0, t, 0)),
        out_shape=jax.ShapeDtypeStruct((planes, batch, dil, seq // dil, A_HEAD_DIM), BF16),
        scratch_shapes=[] if dil == 1 else [pltpu.VMEM((planes, tm, A_HEAD_DIM), F32)],
        compiler_params=_cparams(2),
        name=f"attn_proj_g{gi}",
    )(h, w, w, w, gq, gk)


def _gla_proj_kernel(h_ref, w_ref, o_ref, gl_ref):
    acc = jnp.dot(h_ref[...], w_ref[...], preferred_element_type=F32)
    rows = acc.shape[0]

    def plane(off, width):
        a = acc[:, off:off + width]
        if width < 128:
            a = jnp.concatenate([a, jnp.zeros((rows, 128 - width), F32)], axis=1)
        return a

    for hh in range(GLA_HEADS):
        o_ref[hh] = plane(hh * GLA_DK, GLA_DK).astype(o_ref.dtype)
        o_ref[GLA_HEADS + hh] = plane(GLA_HEADS * GLA_DK + hh * GLA_DK, GLA_DK).astype(o_ref.dtype)
        v_off = 2 * GLA_HEADS * GLA_DK + hh * GLA_DV
        o_ref[2 * GLA_HEADS + 2 * hh] = plane(v_off, 128).astype(o_ref.dtype)
        o_ref[2 * GLA_HEADS + 2 * hh + 1] = plane(v_off + 128, GLA_DV - 128).astype(o_ref.dtype)
    gl_ref[...] = plane(2 * GLA_HEADS * GLA_DK + GLA_HEADS * GLA_DV, GLA_RANK)


def _gla_proj(h, w, l, tm=512):
    m, d = h.shape
    n = w.shape[2]
    n_blk = 4 * GLA_HEADS
    return pl.pallas_call(
        _gla_proj_kernel,
        grid=(m // tm,),
        in_specs=[pl.BlockSpec((tm, d), lambda i: (i, 0)),
                  pl.BlockSpec((None, d, n), lambda i: (l, 0, 0))],
        out_specs=[pl.BlockSpec((n_blk, tm, 128), lambda i: (0, i, 0)),
                   pl.BlockSpec((tm, 128), lambda i: (i, 0))],
        out_shape=[jax.ShapeDtypeStruct((n_blk, m, 128), BF16),
                   jax.ShapeDtypeStruct((m, 128), F32)],
        compiler_params=_cparams(1),
        name="gla_proj",
    )(h, w)


def _sgu_kernel(h_ref, w_ref, lng_ref, lnb_ref, ws_ref, bs_ref, o_ref):
    acc = jnp.dot(h_ref[...], w_ref[...], preferred_element_type=F32)
    u = jax.nn.gelu(acc[:, :SGU_WIDTH])
    gv = jax.nn.gelu(acc[:, SGU_WIDTH:])
    mu = jnp.mean(gv, axis=-1, keepdims=True)
    xc = gv - mu
    var = jnp.mean(xc * xc, axis=-1, keepdims=True)
    vh = (xc * lax.rsqrt(var + EPS) * lng_ref[...] + lnb_ref[...]).astype(BF16)
    half = 2 * SGU_GW
    lane = lax.broadcasted_iota(jnp.int32, (SGU_CHUNK, half), 1)
    first = lane < SGU_GW
    for c in range(h_ref.shape[0] // SGU_CHUNK):
        rows = slice(c * SGU_CHUNK, (c + 1) * SGU_CHUNK)
        for p in range(2):
            cols = slice(p * half, (p + 1) * half)
            vc = vh[rows, cols]
            r0 = jnp.dot(ws_ref[2 * p], vc, preferred_element_type=F32)
            r1 = jnp.dot(ws_ref[2 * p + 1], vc, preferred_element_type=F32)
            f = jnp.where(first, r0, r1) + bs_ref[:, cols]
            o_ref[rows, cols] = (u[rows, cols] * f).astype(o_ref.dtype)


def _sgu(h, w, ln_g, ln_b, ws, bs_tile, l, tm=512):
    m, d = h.shape
    n = w.shape[2]
    return pl.pallas_call(
        _sgu_kernel,
        grid=(m // tm,),
        in_specs=[pl.BlockSpec((tm, d), lambda i: (i, 0)),
                  pl.BlockSpec((None, d, n), lambda i: (l, 0, 0)),
                  pl.BlockSpec((None, 1, SGU_WIDTH), lambda i: (l, 0, 0)),
                  pl.BlockSpec((None, 1, SGU_WIDTH), lambda i: (l, 0, 0)),
                  pl.BlockSpec((None, SGU_GROUPS, SGU_CHUNK, SGU_CHUNK), lambda i: (l, 0, 0, 0)),
                  pl.BlockSpec((None, SGU_CHUNK, SGU_WIDTH), lambda i: (l, 0, 0))],
        out_specs=pl.BlockSpec((tm, SGU_WIDTH), lambda i: (i, 0)),
        out_shape=jax.ShapeDtypeStruct((m, SGU_WIDTH), BF16),
        compiler_params=_cparams(1),
        name="sgu",
    )(h, w, ln_g, ln_b, ws, bs_tile)


def _t5_causal_bucket(dist):
    max_exact = REL_BUCKETS // 2
    d = np.maximum(dist, 1)
    large = max_exact + (np.log(d / max_exact) / np.log(REL_MAX_DIST / max_exact)
                         * (REL_BUCKETS - max_exact)).astype(np.int64)
    large = np.minimum(large, REL_BUCKETS - 1)
    return np.where(dist < max_exact, dist, large).astype(np.int32)


def _attn_bias_tiles(rel_bias):
    i = np.arange(A_BLK)[:, None]
    j = np.arange(2 * A_BLK)[None, :]
    sub_dist = i + A_BLK - j
    valid = (sub_dist >= 0) & (sub_dist <= A_BLK)
    tiles = []
    for gi, dilation in enumerate(DILATIONS):
        bucket = _t5_causal_bucket(np.clip(sub_dist, 0, None) * dilation)
        heads = rel_bias[:, gi * A_HPG:(gi + 1) * A_HPG].astype(F32)
        b = jnp.moveaxis(jnp.take(heads, bucket, axis=0), -1, 0)
        tiles.append(jnp.where(valid[None], b, NEG))
    return jnp.stack(tiles, axis=0)


def _attn_unit(q, k, v, bias, scale):
    s = lax.dot_general(q, k, (((1,), (1,)), ((), ())), preferred_element_type=F32) * scale + bias
    mx = jnp.max(s, axis=-1, keepdims=True)
    p = jnp.exp(s - mx)
    den = jnp.sum(p, axis=-1, keepdims=True)
    o = jnp.dot(p.astype(BF16), v, preferred_element_type=F32) / den
    return o, mx + jnp.log(den)


def _attn_kernel(q0, k0, v0, q1, k1, v1, q2, k2, v2, bias_ref, o_ref, og, lg):
    scale = A_HEAD_DIM ** -0.5
    for gi, (q_ref, k_ref, v_ref) in enumerate(((q0, k0, v0), (q1, k1, v1), (q2, k2, v2))):
        dil = DILATIONS[gi]
        n_blk = q_ref.shape[1] // A_BLK
        bias = bias_ref[gi]
        for r in range(dil):
            for n in range(n_blk):
                q = q_ref[r, n * A_BLK:(n + 1) * A_BLK, :]
                if n == 0:
                    k = k_ref[r, 0:A_BLK, :]
                    v = v_ref[r, 0:A_BLK, :]
                    o, lse = _attn_unit(q, k, v, bias[:, A_BLK:], scale)
                else:
                    k = k_ref[r, (n - 1) * A_BLK:(n + 1) * A_BLK, :]
                    v = v_ref[r, (n - 1) * A_BLK:(n + 1) * A_BLK, :]
                    o, lse = _attn_unit(q, k, v, bias, scale)
                start = r + dil * A_BLK * n
                rows = pl.ds(start, A_BLK) if dil == 1 else pl.ds(start, A_BLK, stride=dil)
                og[gi, rows, :] = o
                lg[gi, rows, :] = jnp.broadcast_to(lse, (A_BLK, A_HEAD_DIM))
    l0, l1, l2 = lg[0], lg[1], lg[2]
    mx = jnp.maximum(jnp.maximum(l0, l1), l2)
    w0, w1, w2 = jnp.exp(l0 - mx), jnp.exp(l1 - mx), jnp.exp(l2 - mx)
    o = (w0 * og[0] + w1 * og[1] + w2 * og[2]) / (w0 + w1 + w2)
    o_ref[...] = o.astype(o_ref.dtype)


def _attention(qkvs, bias, batch, seq):
    in_specs = []
    args = []
    for gi, qkv in enumerate(qkvs):
        dil = DILATIONS[gi]
        for which in range(3):
            in_specs.append(pl.BlockSpec((None, None, dil, seq // dil, A_HEAD_DIM),
                                         lambda b, hh, which=which: (which * A_HPG + hh, b, 0, 0, 0)))
            args.append(qkv)
    in_specs.append(pl.BlockSpec((N_A_GROUPS, None, A_BLK, 2 * A_BLK), lambda b, hh: (0, hh, 0, 0)))
    return pl.pallas_call(
        _attn_kernel,
        grid=(batch, A_HPG),
        in_specs=in_specs,
        out_specs=pl.BlockSpec((None, seq, A_HEAD_DIM), lambda b, hh: (hh, b, 0)),
        out_shape=jax.ShapeDtypeStruct((A_HPG, batch * seq, A_HEAD_DIM), BF16),
        scratch_shapes=[pltpu.VMEM((N_A_GROUPS, seq, A_HEAD_DIM), F32),
                        pltpu.VMEM((N_A_GROUPS, seq, A_HEAD_DIM), F32)],
        compiler_params=_cparams(2),
        name="attention",
    )(*args, bias)


def _gla_static_matrices(tile):
    t = np.arange(tile)
    same = (t[:, None] // GLA_CHUNK) == (t[None, :] // GLA_CHUNK)
    inc = same & (t[None, :] <= t[:, None])
    a = np.arange(GLA_ATILE)
    ti, si = a[:, None], a[None, :]
    masks = []
    for m in GLA_LEVELS:
        masks.append((ti // (2 * m) == si // (2 * m)) & ((ti // m) % 2 == 1) & ((si // m) % 2 == 0))
    return jnp.asarray(inc, BF16), jnp.asarray(np.stack(masks), F32)


def _gla_kernel(q_ref, k_ref, v_ref, gl_ref, wh_ref, wl_ref, bg_ref, gout_ref, inc_ref, lvl_ref,
                o_ref, st_ref):
    @pl.when(pl.program_id(1) == 0)
    def _():
        st_ref[...] = jnp.zeros_like(st_ref)

    t_rows = gl_ref.shape[0]
    n_sub = t_rows // 8
    width = GLA_HEADS * GLA_DK_PAD

    def split(x):
        hi = x.astype(BF16)
        return hi, (x - hi.astype(F32)).astype(BF16)

    g_hi, g_lo = split(gl_ref[...])
    wh, wl = wh_ref[...], wl_ref[...]
    x = (jnp.dot(g_hi, wh, preferred_element_type=F32) + jnp.dot(g_hi, wl, preferred_element_type=F32)
         + jnp.dot(g_lo, wh, preferred_element_type=F32) + bg_ref[...])
    la = (jnp.minimum(x, 0.0) - jnp.log1p(jnp.exp(-jnp.abs(x)))) * (1.0 / GLA_TAU)
    la_hi, la_lo = split(la)
    inc = inc_ref[...]
    b = jnp.dot(inc, la_hi, preferred_element_type=F32) + jnp.dot(inc, la_lo, preferred_element_type=F32)

    b3 = b.reshape(n_sub, 8, width)
    la3 = la.reshape(n_sub, 8, width)
    sub = lax.broadcasted_iota(jnp.int32, (1, 8, width), 1)

    def hi_step(y, bit):
        return jnp.where((sub & bit) == 0, pltpu.roll(y, 8 - bit, axis=1), y)

    def lo_step(y, bit):
        return jnp.where((sub & bit) != 0, pltpu.roll(y, bit, axis=1), y)

    def spread(y, groups, pick):
        y4 = y.reshape(n_sub // groups, groups, 8, width)
        return jnp.broadcast_to(y4[:, pick:pick + 1], y4.shape).reshape(n_sub, 8, width)

    hi = {1: b3}
    lo = {1: b3 - la3}
    for bit in (1, 2, 4):
        hi[2 * bit] = hi_step(hi[bit], bit)
        lo[2 * bit] = lo_step(lo[bit], bit)
    for groups in (2, 4, 8):
        hi[8 * groups] = spread(hi[8], groups, groups - 1)
        lo[8 * groups] = spread(lo[8], groups, 0)

    q_all = jnp.concatenate([q_ref[h] for h in range(GLA_HEADS)], axis=1).astype(F32) * (GLA_DK ** -0.5)
    k_all = jnp.concatenate([k_ref[h] for h in range(GLA_HEADS)], axis=1).astype(F32)
    q3 = q_all.reshape(n_sub, 8, width)
    k3 = k_all.reshape(n_sub, 8, width)

    def q_side(m):
        return (q3 * jnp.exp(b3 - lo[m])).reshape(t_rows, width).astype(BF16)

    def k_side(m):
        return (k3 * jnp.exp(hi[m] - b3)).reshape(t_rows, width).astype(BF16)

    qd = {m: q_side(m) for m in GLA_LEVELS + (GLA_CHUNK,)}
    kd = {m: (k_all.astype(BF16) if m == 1 else k_side(m)) for m in GLA_LEVELS + (GLA_CHUNK,)}
    diag = q_all * k_all
    v_heads = [jnp.concatenate([v_ref[2 * h], v_ref[2 * h + 1]], axis=1) for h in range(GLA_HEADS)]
    g_out = gout_ref[...]
    nt = (((1,), (1,)), ((), ()))
    chunks_per_atile = GLA_ATILE // GLA_CHUNK

    for a in range(t_rows // GLA_ATILE):
        arows = slice(a * GLA_ATILE, (a + 1) * GLA_ATILE)
        for h in range(GLA_HEADS):
            cols = slice(h * GLA_DK_PAD, (h + 1) * GLA_DK_PAD)
            scores = None
            for li, m in enumerate(GLA_LEVELS):
                s = lax.dot_general(qd[m][arows, cols], kd[m][arows, cols], nt, preferred_element_type=F32)
                s = s * lvl_ref[li]
                scores = s if scores is None else scores + s
            v_a = v_heads[h][arows]
            rd = jnp.sum(diag[arows, cols], axis=-1, keepdims=True)
            intra = jnp.dot(scores.astype(BF16), v_a, preferred_element_type=F32) + rd * v_a.astype(F32)
            for cc in range(chunks_per_atile):
                c = a * chunks_per_atile + cc
                rows = slice(c * GLA_CHUNK, (c + 1) * GLA_CHUNK)
                st = st_ref[h]
                inter = lax.dot_general(qd[GLA_CHUNK][rows, cols], st.astype(BF16), nt,
                                        preferred_element_type=F32)
                o = intra[cc * GLA_CHUNK:(cc + 1) * GLA_CHUNK] + inter
                ms = jnp.sum(o * o, axis=-1, keepdims=True) * (1.0 / GLA_DV)
                o_ref[rows, h * GLA_DV_PAD:(h + 1) * GLA_DV_PAD] = (
                    o * lax.rsqrt(ms + EPS) * g_out).astype(o_ref.dtype)
                last = c * GLA_CHUNK + GLA_CHUNK - 1
                decay = jnp.exp(b[last:last + 1, cols])
                upd = lax.dot_general(v_heads[h][rows], kd[GLA_CHUNK][rows, cols],
                                      (((0,), (0,)), ((), ())), preferred_element_type=F32)
                st_ref[h] = st * decay + upd


def _gla(glaqkv, gl, wup_hi, wup_lo, bg, gout, l, batch, seq):
    m = gl.shape[0]
    tile = GLA_TILE
    tiles_per_seq = seq // tile
    inc, lvl = _gla_static_matrices(tile)
    row = lambda b, t: b * tiles_per_seq + t
    width = GLA_HEADS * GLA_DK_PAD
    return pl.pallas_call(
        _gla_kernel,
        grid=(batch, tiles_per_seq),
        in_specs=[pl.BlockSpec((GLA_HEADS, tile, 128), lambda b, t: (0, row(b, t), 0)),
                  pl.BlockSpec((GLA_HEADS, tile, 128), lambda b, t: (1, row(b, t), 0)),
                  pl.BlockSpec((2 * GLA_HEADS, tile, 128), lambda b, t: (1, row(b, t), 0)),
                  pl.BlockSpec((tile, 128), lambda b, t: (row(b, t), 0)),
                  pl.BlockSpec((None, 128, width), lambda b, t: (l, 0, 0)),
                  pl.BlockSpec((None, 128, width), lambda b, t: (l, 0, 0)),
                  pl.BlockSpec((None, 1, width), lambda b, t: (l, 0, 0)),
                  pl.BlockSpec((None, 1, GLA_DV_PAD), lambda b, t: (l, 0, 0)),
                  pl.BlockSpec((tile, tile), lambda b, t: (0, 0)),
                  pl.BlockSpec((len(GLA_LEVELS), GLA_ATILE, GLA_ATILE), lambda b, t: (0, 0, 0))],
        out_specs=pl.BlockSpec((tile, GLA_HEADS * GLA_DV_PAD), lambda b, t: (row(b, t), 0)),
        out_shape=jax.ShapeDtypeStruct((m, GLA_HEADS * GLA_DV_PAD), BF16),
        scratch_shapes=[pltpu.VMEM((GLA_HEADS, GLA_DV_PAD, GLA_DK_PAD), F32)],
        compiler_params=_cparams(2),
        name="gla",
    )(glaqkv, glaqkv, glaqkv, gl, wup_hi, wup_lo, bg, gout, inc, lvl)


def _merge_kernel(oa_ref, ob_ref, oc_ref, ga_ref, gb_ref, gc_ref, pa_ref, pb_ref, pc_ref, y_ref):
    oa = jnp.concatenate([oa_ref[hh] for hh in range(A_HPG)], axis=1)
    ya = jnp.dot(oa, pa_ref[...], preferred_element_type=F32)
    yb = jnp.dot(ob_ref[...], pb_ref[...], preferred_element_type=F32)
    yc = jnp.dot(oc_ref[...], pc_ref[...], preferred_element_type=F32)
    y = (ga_ref[...].astype(F32) * ya + gb_ref[...].astype(F32) * yb + gc_ref[...].astype(F32) * yc)
    y_ref[...] = y.astype(y_ref.dtype)


def _merge(oa, ob, oc, gates, pa, pb, pc, l, tm=256):
    m = ob.shape[0]
    n = pa.shape[2]
    const = lambda i: (l, 0, 0)
    return pl.pallas_call(
        _merge_kernel,
        grid=(m // tm,),
        in_specs=[pl.BlockSpec((A_HPG, tm, A_HEAD_DIM), lambda i: (0, i, 0)),
                  pl.BlockSpec((tm, ob.shape[1]), lambda i: (i, 0)),
                  pl.BlockSpec((tm, oc.shape[1]), lambda i: (i, 0)),
                  pl.BlockSpec((tm, n), lambda i: (i, 0)),
                  pl.BlockSpec((tm, n), lambda i: (i, 1)),
                  pl.BlockSpec((tm, n), lambda i: (i, 2)),
                  pl.BlockSpec((None,) + pa.shape[1:], const),
                  pl.BlockSpec((None,) + pb.shape[1:], const),
                  pl.BlockSpec((None,) + pc.shape[1:], const)],
        out_specs=pl.BlockSpec((tm, n), lambda i: (i, 0)),
        out_shape=jax.ShapeDtypeStruct((m, n), BF16),
        compiler_params=_cparams(1),
        name="merge",
    )(oa, ob, oc, gates, gates, gates, pa, pb, pc)


def _mm_res_kernel(a_ref, w_ref, x_ref, o_ref):
    k = pl.program_id(2)
    acc = jnp.dot(a_ref[...], w_ref[...], preferred_element_type=F32)

    @pl.when(k == 0)
    def _():
        o_ref[...] = x_ref[...] + acc

    @pl.when(k > 0)
    def _():
        o_ref[...] += acc


def _mm_res(a, w, x, l, tm=1024, tn=1024, tk=None):
    m, kdim = a.shape
    n = w.shape[2]
    tk = kdim if tk is None else tk
    return pl.pallas_call(
        _mm_res_kernel,
        grid=(m // tm, n // tn, kdim // tk),
        in_specs=[pl.BlockSpec((tm, tk), lambda i, j, k: (i, k)),
                  pl.BlockSpec((None, tk, tn), lambda i, j, k: (l, k, j)),
                  pl.BlockSpec((tm, tn), lambda i, j, k: (i, j))],
        out_specs=pl.BlockSpec((tm, tn), lambda i, j, k: (i, j)),
        out_shape=jax.ShapeDtypeStruct((m, n), F32),
        compiler_params=_cparams(3),
        name="mm_res",
    )(a, w, x)


def _ffn_in_kernel(x_ref, g_ref, wg_ref, wu_ref, o_ref, h_scr):
    def tile(h):
        g = jnp.dot(h, wg_ref[...], preferred_element_type=F32)
        u = jnp.dot(h, wu_ref[...], preferred_element_type=F32)
        return (g * jax.nn.sigmoid(g) * u).astype(o_ref.dtype)

    @pl.when(pl.program_id(1) == 0)
    def _():
        for c in range(x_ref.shape[0] // NORM_CHUNK):
            rows = slice(c * NORM_CHUNK, (c + 1) * NORM_CHUNK)
            h = _rms(x_ref[rows, :], g_ref[...]).astype(h_scr.dtype)
            h_scr[rows, :] = h
            o_ref[rows, :] = tile(h)

    @pl.when(pl.program_id(1) > 0)
    def _():
        o_ref[...] = tile(h_scr[...])


def _ffn_in(x, g, w, l, tm=1024, tn=512):
    m, d = x.shape
    f = w.shape[2] // 2
    up_off = f // tn
    return pl.pallas_call(
        _ffn_in_kernel,
        grid=(m // tm, f // tn),
        in_specs=[pl.BlockSpec((tm, d), lambda i, j: (i, 0)),
                  pl.BlockSpec((None, 1, d), lambda i, j: (l, 0, 0)),
                  pl.BlockSpec((None, d, tn), lambda i, j: (l, 0, j)),
                  pl.BlockSpec((None, d, tn), lambda i, j: (l, 0, up_off + j))],
        out_specs=pl.BlockSpec((tm, tn), lambda i, j: (i, j)),
        out_shape=jax.ShapeDtypeStruct((m, f), BF16),
        scratch_shapes=[pltpu.VMEM((tm, d), BF16)],
        compiler_params=_cparams(2),
        name="ffn_in",
    )(x, g, w, w)


def _pad_heads(w, off, n_heads, width, padded):
    lead = w.shape[:-1]
    cols = w[..., off:off + n_heads * width].reshape(lead + (n_heads, width))
    cols = jnp.pad(cols, [(0, 0)] * (len(lead) + 1) + [(0, padded - width)])
    return cols.reshape(lead + (n_heads * padded,))


def _prepare_params(w_in, gla_gate_up, gla_gate_b, gla_out_g, sgu_w, sgu_b, w_branch):
    depth = w_in.shape[0]
    w_attn = w_in[:, :, _OFF_AQ:_OFF_BQ].astype(BF16)
    n_gla = _OFF_CU - _OFF_BQ
    w_gla = jnp.pad(w_in[:, :, _OFF_BQ:_OFF_CU].astype(BF16), ((0, 0), (0, 0), (0, -n_gla % 128)))
    w_sgu = w_in[:, :, _OFF_CU:_OFF_GATES].astype(BF16)
    w_gates = w_in[:, :, _OFF_GATES:].astype(BF16)
    wup = _pad_heads(gla_gate_up, 0, GLA_HEADS, GLA_DK, GLA_DK_PAD)
    wup = jnp.pad(wup, ((0, 0), (0, 128 - GLA_RANK), (0, 0)))
    wup_hi = wup.astype(BF16)
    wup_lo = (wup - wup_hi.astype(F32)).astype(BF16)
    bg = _pad_heads(gla_gate_b.reshape(depth, 1, -1), 0, GLA_HEADS, GLA_DK, GLA_DK_PAD)
    gout = jnp.pad(gla_out_g.reshape(depth, 1, GLA_DV), ((0, 0), (0, 0), (0, GLA_DV_PAD - GLA_DV)))
    tril = np.tril(np.ones((SGU_CHUNK, SGU_CHUNK), dtype=bool))
    ws = jnp.where(tril, sgu_w, 0.0).astype(BF16)
    bs_tile = jnp.repeat(jnp.swapaxes(sgu_b, 1, 2), SGU_GW, axis=2)
    pa = w_branch[:, :A_WIDTH].astype(BF16)
    pb = w_branch[:, A_WIDTH:A_WIDTH + GLA_HEADS * GLA_DV].reshape(depth, GLA_HEADS, GLA_DV, D_MODEL)
    pb = jnp.pad(pb, ((0, 0), (0, 0), (0, GLA_DV_PAD - GLA_DV), (0, 0)))
    pb = pb.reshape(depth, GLA_HEADS * GLA_DV_PAD, D_MODEL).astype(BF16)
    pc = w_branch[:, A_WIDTH + GLA_HEADS * GLA_DV:].astype(BF16)
    return dict(w_attn=w_attn, w_gla=w_gla, w_sgu=w_sgu, w_gates=w_gates, wup_hi=wup_hi, wup_lo=wup_lo,
                bg=bg, gout=gout, ws=ws, bs_tile=bs_tile, pa=pa, pb=pb, pc=pc)


def kernel(x, rel_bias, norm1_g, w_in, q_norm_g, k_norm_g, gla_gate_up, gla_gate_b, gla_out_g,
           sgu_ln_g, sgu_ln_b, sgu_w, sgu_b, w_branch, w_out, norm2_g, w_ffn_in, w_ffn_out):
    batch, seq, d = x.shape
    depth = w_in.shape[0]
    m = batch * seq
    assert seq % (A_BLK * max(DILATIONS)) == 0 and seq % GLA_TILE == 0 and d == D_MODEL
    p = _prepare_params(w_in, gla_gate_up, gla_gate_b, gla_out_g, sgu_w, sgu_b, w_branch)
    w_out_b = w_out.astype(BF16)
    w_ffn_in_b = w_ffn_in.astype(BF16)
    w_ffn_out_b = w_ffn_out.astype(BF16)
    n1 = norm1_g.reshape(depth, 1, d)
    n2 = norm2_g.reshape(depth, 1, d)
    gq = q_norm_g.reshape(depth, 1, A_HEAD_DIM)
    gk = k_norm_g.reshape(depth, 1, A_HEAD_DIM)
    ln_g = sgu_ln_g.reshape(depth, 1, SGU_WIDTH)
    ln_b = sgu_ln_b.reshape(depth, 1, SGU_WIDTH)
    bias = _attn_bias_tiles(rel_bias)
    xf = x.reshape(m, d)
    for l in range(depth):
        gates, h = _gates(xf, n1, p["w_gates"], l)
        qkvs = [_attn_proj(h, p["w_attn"], gq, gk, l, gi, batch, seq) for gi in range(N_A_GROUPS)]
        glaqkv, gl = _gla_proj(h, p["w_gla"], l)
        o_c = _sgu(h, p["w_sgu"], ln_g, ln_b, p["ws"], p["bs_tile"], l)
        o_a = _attention(qkvs, bias, batch, seq)
        o_b = _gla(glaqkv, gl, p["wup_hi"], p["wup_lo"], p["bg"], p["gout"], l, batch, seq)
        y = _merge(o_a, o_b, o_c, gates, p["pa"], p["pb"], p["pc"], l)
        xf = _mm_res(y, w_out_b, xf, l, tm=512, tn=D_MODEL)
        a = _ffn_in(xf, n2, w_ffn_in_b, l)
        xf = _mm_res(a, w_ffn_out_b, xf, l, tk=D_FFN // 2)
    return xf.reshape(batch, seq, d)
```

```python
import functools

import numpy as np
import jax
import jax.numpy as jnp
from jax import lax
from jax.experimental import pallas as pl
from jax.experimental.pallas import tpu as pltpu

F32 = jnp.float32
BF16 = jnp.bfloat16

D_MODEL = 2048
DILATIONS = (1, 4, 16)
N_A_GROUPS = 3
A_HEAD_DIM = 128
A_WIDTH = 512
A_HPG = 4
A_BLK = 128
GLA_HEADS = 4
GLA_DK = 96
GLA_DV = 192
GLA_DK_PAD = 128
GLA_DV_PAD = 256
GLA_RANK = 16
GLA_TAU = 16.0
GLA_CHUNK = 64
GLA_LEVELS = (1, 2, 4, 8, 16, 32)
GLA_ATILE = 128
SGU_WIDTH = 768
SGU_GROUPS = 4
SGU_GW = SGU_WIDTH // SGU_GROUPS
SGU_CHUNK = 128
D_FFN = 5632
REL_BUCKETS = 32
REL_MAX_DIST = 2048
EPS = 1e-6
NEG = -1e30

_OFF_AQ, _OFF_AK, _OFF_AV = 0, 1536, 3072
_OFF_BQ, _OFF_BK, _OFF_BV = 4608, 4992, 5376
_OFF_GL = 6144
_OFF_CU, _OFF_CV = 6160, 6928
_OFF_GATES = 7696

VMEM_LIMIT_BYTES = 56 * 1024 * 1024

GLA_TILE = 256
NORM_CHUNK = 256
GATE_CHUNK = 1024


def _cparams(n_axes):
    return pltpu.CompilerParams(
        dimension_semantics=("arbitrary",) * n_axes,
        vmem_limit_bytes=VMEM_LIMIT_BYTES,
    )


def _rms(x, g):
    ms = jnp.mean(x * x, axis=-1, keepdims=True)
    return x * lax.rsqrt(ms + EPS) * g


def _attn_proj_kernel(h_ref, wq_ref, wk_ref, wv_ref, gq_ref, gk_ref, o_ref, *scratch, dil):
    h = h_ref[...]
    sub_rows = h.shape[0] // dil
    for which, (w_ref, g_ref) in enumerate(((wq_ref, gq_ref), (wk_ref, gk_ref), (wv_ref, None))):
        acc = jnp.dot(h, w_ref[...], preferred_element_type=F32)
        for hh in range(A_HPG):
            p = which * A_HPG + hh
            a = acc[:, hh * A_HEAD_DIM:(hh + 1) * A_HEAD_DIM]
            if g_ref is not None:
                a = _rms(a, g_ref[...])
            if dil == 4:
                scr = scratch[0]
                scr[p] = a
                for r in range(dil):
                    o_ref[p, r] = scr[p, pl.ds(r, sub_rows, stride=dil), :].astype(o_ref.dtype)
            else:
                scr, scr2 = scratch
                scr[p] = a
                quarter = h.shape[0] // 4
                for r1 in range(4):
                    scr2[p, r1] = scr[p, pl.ds(r1, quarter, stride=4), :]
                    for r2 in range(4):
                        o_ref[p, r1 + 4 * r2] = scr2[p, r1, pl.ds(r2, sub_rows, stride=4), :].astype(o_ref.dtype)


def _attn_proj(h, w, gq, gk, l, gi, batch, seq, tm=1024):
    d_model = h.shape[1]
    dil = DILATIONS[gi]
    planes = 3 * A_HPG
    tiles_per_seq = seq // tm
    w_spec = lambda which: pl.BlockSpec((None, d_model, A_WIDTH),
                                        lambda b, t: (l, 0, which * N_A_GROUPS + gi))
    return pl.pallas_call(
        functools.partial(_attn_proj_kernel, dil=dil),
        grid=(batch, tiles_per_seq),
        in_specs=[pl.BlockSpec((tm, d_model), lambda b, t: (b * tiles_per_seq + t, 0)),
                  w_spec(0), w_spec(1), w_spec(2),
                  pl.BlockSpec((None, 1, A_HEAD_DIM), lambda b, t: (l, 0, 0)),
                  pl.BlockSpec((None, 1, A_HEAD_DIM), lambda b, t: (l, 0, 0))],
        out_specs=pl.BlockSpec((planes, None, dil, tm // dil, A_HEAD_DIM), lambda b, t: (0, b, 0, t, 0)),
        out_shape=jax.ShapeDtypeStruct((planes, batch, dil, seq // dil, A_HEAD_DIM), BF16),
        scratch_shapes=([pltpu.VMEM((planes, tm, A_HEAD_DIM), F32)]
                        + ([pltpu.VMEM((planes, 4, tm // 4, A_HEAD_DIM), F32)] if dil == 16 else [])),
        compiler_params=_cparams(2),
        name=f"attn_proj_g{gi}",
    )(h, w, w, w, gq, gk)


def _attn_proj_norm_kernel(x_ref, ng_ref, wq_ref, wk_ref, wv_ref, gq_ref, gk_ref, o_ref, h_ref):
    for c in range(x_ref.shape[0] // NORM_CHUNK):
        rows = slice(c * NORM_CHUNK, (c + 1) * NORM_CHUNK)
        h = _rms(x_ref[rows, :], ng_ref[...]).astype(h_ref.dtype)
        h_ref[rows, :] = h
        for which, (w_ref, g_ref) in enumerate(((wq_ref, gq_ref), (wk_ref, gk_ref), (wv_ref, None))):
            acc = jnp.dot(h, w_ref[...], preferred_element_type=F32)
            for hh in range(A_HPG):
                a = acc[:, hh * A_HEAD_DIM:(hh + 1) * A_HEAD_DIM]
                if g_ref is not None:
                    a = _rms(a, g_ref[...])
                o_ref[which * A_HPG + hh, 0, rows, :] = a.astype(o_ref.dtype)


def _attn_proj_norm(x, ng, w, gq, gk, l, batch, seq, tm=1024):
    m, d_model = x.shape
    planes = 3 * A_HPG
    tiles_per_seq = seq // tm
    w_spec = lambda which: pl.BlockSpec((None, d_model, A_WIDTH), lambda b, t: (l, 0, which * N_A_GROUPS))
    row = lambda b, t: b * tiles_per_seq + t
    return pl.pallas_call(
        _attn_proj_norm_kernel,
        grid=(batch, tiles_per_seq),
        in_specs=[pl.BlockSpec((tm, d_model), lambda b, t: (row(b, t), 0)),
                  pl.BlockSpec((None, 1, d_model), lambda b, t: (l, 0, 0)),
                  w_spec(0), w_spec(1), w_spec(2),
                  pl.BlockSpec((None, 1, A_HEAD_DIM), lambda b, t: (l, 0, 0)),
                  pl.BlockSpec((None, 1, A_HEAD_DIM), lambda b, t: (l, 0, 0))],
        out_specs=[pl.BlockSpec((planes, None, 1, tm, A_HEAD_DIM), lambda b, t: (0, b, 0, t, 0)),
                   pl.BlockSpec((tm, d_model), lambda b, t: (row(b, t), 0))],
        out_shape=[jax.ShapeDtypeStruct((planes, batch, 1, seq, A_HEAD_DIM), BF16),
                   jax.ShapeDtypeStruct((m, d_model), BF16)],
        compiler_params=_cparams(2),
        name="attn_proj_norm_g0",
    )(x, ng, w, w, w, gq, gk)


def _gla_proj_kernel(h_ref, w_ref, o_ref, gl_ref):
    acc = jnp.dot(h_ref[...], w_ref[...], preferred_element_type=F32)
    rows = acc.shape[0]

    def plane(off, width):
        a = acc[:, off:off + width]
        if width < 128:
            a = jnp.concatenate([a, jnp.zeros((rows, 128 - width), F32)], axis=1)
        return a

    for hh in range(GLA_HEADS):
        o_ref[hh] = plane(hh * GLA_DK, GLA_DK).astype(o_ref.dtype)
        o_ref[GLA_HEADS + hh] = plane(GLA_HEADS * GLA_DK + hh * GLA_DK, GLA_DK).astype(o_ref.dtype)
        v_off = 2 * GLA_HEADS * GLA_DK + hh * GLA_DV
        o_ref[2 * GLA_HEADS + 2 * hh] = plane(v_off, 128).astype(o_ref.dtype)
        o_ref[2 * GLA_HEADS + 2 * hh + 1] = plane(v_off + 128, GLA_DV - 128).astype(o_ref.dtype)
    gl_ref[...] = plane(2 * GLA_HEADS * GLA_DK + GLA_HEADS * GLA_DV, GLA_RANK)


def _gla_proj(h, w, l, tm=512):
    m, d = h.shape
    n = w.shape[2]
    n_blk = 4 * GLA_HEADS
    return pl.pallas_call(
        _gla_proj_kernel,
        grid=(m // tm,),
        in_specs=[pl.BlockSpec((tm, d), lambda i: (i, 0)),
                  pl.BlockSpec((None, d, n), lambda i: (l, 0, 0))],
        out_specs=[pl.BlockSpec((n_blk, tm, 128), lambda i: (0, i, 0)),
                   pl.BlockSpec((tm, 128), lambda i: (i, 0))],
        out_shape=[jax.ShapeDtypeStruct((n_blk, m, 128), BF16),
                   jax.ShapeDtypeStruct((m, 128), F32)],
        compiler_params=_cparams(1),
        name="gla_proj",
    )(h, w)


def _sgu_kernel(h_ref, w_ref, lng_ref, lnb_ref, ws_ref, bs_ref, o_ref):
    acc = jnp.dot(h_ref[...], w_ref[...], preferred_element_type=F32)
    u = jax.nn.gelu(acc[:, :SGU_WIDTH])
    gv = jax.nn.gelu(acc[:, SGU_WIDTH:])
    mu = jnp.mean(gv, axis=-1, keepdims=True)
    xc = gv - mu
    var = jnp.mean(xc * xc, axis=-1, keepdims=True)
    vh = (xc * lax.rsqrt(var + EPS) * lng_ref[...] + lnb_ref[...]).astype(BF16)
    half = 2 * SGU_GW
    lane = lax.broadcasted_iota(jnp.int32, (SGU_CHUNK, half), 1)
    first = lane < SGU_GW
    for c in range(h_ref.shape[0] // SGU_CHUNK):
        rows = slice(c * SGU_CHUNK, (c + 1) * SGU_CHUNK)
        for p in range(2):
            cols = slice(p * half, (p + 1) * half)
            vc = vh[rows, cols]
            r0 = jnp.dot(ws_ref[2 * p], vc, preferred_element_type=F32)
            r1 = jnp.dot(ws_ref[2 * p + 1], vc, preferred_element_type=F32)
            f = jnp.where(first, r0, r1) + bs_ref[:, cols]
            o_ref[rows, cols] = (u[rows, cols] * f).astype(o_ref.dtype)


def _sgu(h, w, ln_g, ln_b, ws, bs_tile, l, tm=512):
    m, d = h.shape
    n = w.shape[2]
    return pl.pallas_call(
        _sgu_kernel,
        grid=(m // tm,),
        in_specs=[pl.BlockSpec((tm, d), lambda i: (i, 0)),
                  pl.BlockSpec((None, d, n), lambda i: (l, 0, 0)),
                  pl.BlockSpec((None, 1, SGU_WIDTH), lambda i: (l, 0, 0)),
                  pl.BlockSpec((None, 1, SGU_WIDTH), lambda i: (l, 0, 0)),
                  pl.BlockSpec((None, SGU_GROUPS, SGU_CHUNK, SGU_CHUNK), lambda i: (l, 0, 0, 0)),
                  pl.BlockSpec((None, SGU_CHUNK, SGU_WIDTH), lambda i: (l, 0, 0))],
        out_specs=pl.BlockSpec((tm, SGU_WIDTH), lambda i: (i, 0)),
        out_shape=jax.ShapeDtypeStruct((m, SGU_WIDTH), BF16),
        compiler_params=_cparams(1),
        name="sgu",
    )(h, w, ln_g, ln_b, ws, bs_tile)


def _t5_causal_bucket(dist):
    max_exact = REL_BUCKETS // 2
    d = np.maximum(dist, 1)
    large = max_exact + (np.log(d / max_exact) / np.log(REL_MAX_DIST / max_exact)
                         * (REL_BUCKETS - max_exact)).astype(np.int64)
    large = np.minimum(large, REL_BUCKETS - 1)
    return np.where(dist < max_exact, dist, large).astype(np.int32)


def _attn_bias_tiles(rel_bias):
    dist = np.arange(A_BLK + 1)
    period = 2 * A_BLK + 1
    tiles = []
    for gi, dilation in enumerate(DILATIONS):
        onehot = np.eye(REL_BUCKETS, dtype=np.float32)[_t5_causal_bucket(dist * dilation)]
        heads = rel_bias[:, gi * A_HPG:(gi + 1) * A_HPG].astype(F32)
        vec = jnp.dot(jnp.asarray(onehot), heads, precision=lax.Precision.HIGHEST).T
        seq = jnp.concatenate([vec[:, ::-1], jnp.full((A_HPG, period - A_BLK - 1), NEG, F32)], axis=1)
        flat = jnp.tile(seq, (1, A_BLK))[:, :A_BLK * 2 * A_BLK]
        tiles.append(flat.reshape(A_HPG, A_BLK, 2 * A_BLK))
    return jnp.stack(tiles, axis=0)


def _attn_unit(q, k, v, bias, scale):
    s = lax.dot_general(q, k, (((1,), (1,)), ((), ())), preferred_element_type=F32) * scale + bias
    mx = jnp.max(s, axis=-1, keepdims=True)
    p = jnp.exp(s - mx)
    den = jnp.sum(p, axis=-1, keepdims=True)
    o = jnp.dot(p.astype(BF16), v, preferred_element_type=F32) / den
    return o, mx + jnp.log(den)


def _attn_kernel(q0, k0, v0, q1, k1, v1, q2, k2, v2, bias_ref, o_ref, og, lg):
    scale = A_HEAD_DIM ** -0.5
    for gi, (q_ref, k_ref, v_ref) in enumerate(((q0, k0, v0), (q1, k1, v1), (q2, k2, v2))):
        dil = DILATIONS[gi]
        n_blk = q_ref.shape[1] // A_BLK
        bias = bias_ref[gi]
        for r in range(dil):
            for n in range(n_blk):
                q = q_ref[r, n * A_BLK:(n + 1) * A_BLK, :]
                if n == 0:
                    k = k_ref[r, 0:A_BLK, :]
                    v = v_ref[r, 0:A_BLK, :]
                    o, lse = _attn_unit(q, k, v, bias[:, A_BLK:], scale)
                else:
                    k = k_ref[r, (n - 1) * A_BLK:(n + 1) * A_BLK, :]
                    v = v_ref[r, (n - 1) * A_BLK:(n + 1) * A_BLK, :]
                    o, lse = _attn_unit(q, k, v, bias, scale)
                start = r + dil * A_BLK * n
                rows = pl.ds(start, A_BLK) if dil == 1 else pl.ds(start, A_BLK, stride=dil)
                og[gi, rows, :] = o
                lg[gi, rows, :] = jnp.broadcast_to(lse, (A_BLK, A_HEAD_DIM))
    l0, l1, l2 = lg[0], lg[1], lg[2]
    mx = jnp.maximum(jnp.maximum(l0, l1), l2)
    w0, w1, w2 = jnp.exp(l0 - mx), jnp.exp(l1 - mx), jnp.exp(l2 - mx)
    o = (w0 * og[0] + w1 * og[1] + w2 * og[2]) / (w0 + w1 + w2)
    o_ref[...] = o.astype(o_ref.dtype)


def _attention(qkvs, bias, batch, seq):
    in_specs = []
    args = []
    for gi, qkv in enumerate(qkvs):
        dil = DILATIONS[gi]
        for which in range(3):
            in_specs.append(pl.BlockSpec((None, None, dil, seq // dil, A_HEAD_DIM),
                                         lambda b, hh, which=which: (which * A_HPG + hh, b, 0, 0, 0)))
            args.append(qkv)
    in_specs.append(pl.BlockSpec((N_A_GROUPS, None, A_BLK, 2 * A_BLK), lambda b, hh: (0, hh, 0, 0)))
    return pl.pallas_call(
        _attn_kernel,
        grid=(batch, A_HPG),
        in_specs=in_specs,
        out_specs=pl.BlockSpec((None, seq, A_HEAD_DIM), lambda b, hh: (hh, b, 0)),
        out_shape=jax.ShapeDtypeStruct((A_HPG, batch * seq, A_HEAD_DIM), BF16),
        scratch_shapes=[pltpu.VMEM((N_A_GROUPS, seq, A_HEAD_DIM), F32),
                        pltpu.VMEM((N_A_GROUPS, seq, A_HEAD_DIM), F32)],
        compiler_params=_cparams(2),
        name="attention",
    )(*args, bias)


def _gla_static_matrices(tile):
    t = np.arange(tile)
    same = (t[:, None] // GLA_CHUNK) == (t[None, :] // GLA_CHUNK)
    inc = same & (t[None, :] <= t[:, None])
    a = np.arange(GLA_ATILE)
    ti, si = a[:, None], a[None, :]
    masks = []
    for m in GLA_LEVELS:
        masks.append((ti // (2 * m) == si // (2 * m)) & ((ti // m) % 2 == 1) & ((si // m) % 2 == 0))
    return jnp.asarray(inc, BF16), jnp.asarray(np.stack(masks), F32)


def _gla_kernel(q_ref, k_ref, v_ref, gl_ref, wh_ref, wl_ref, bg_ref, gout_ref, inc_ref, lvl_ref,
                h_ref, wg_ref, o_ref, gates_ref, st_ref):
    @pl.when(pl.program_id(1) == 0)
    def _():
        st_ref[...] = jnp.zeros_like(st_ref)

    n_gate_chunks = gates_ref.shape[1] // GATE_CHUNK

    def gate_chunks(first, last):
        for c in range(first, last):
            cols = slice(c * GATE_CHUNK, (c + 1) * GATE_CHUNK)
            acc = jnp.dot(h_ref[...], wg_ref[:, cols], preferred_element_type=F32)
            gates_ref[:, cols] = jax.nn.sigmoid(acc).astype(gates_ref.dtype)

    t_rows = gl_ref.shape[0]
    n_sub = t_rows // 8
    width = GLA_HEADS * GLA_DK_PAD

    def split(x):
        hi = x.astype(BF16)
        return hi, (x - hi.astype(F32)).astype(BF16)

    g_hi, g_lo = split(gl_ref[...])
    wh, wl = wh_ref[...], wl_ref[...]
    x = (jnp.dot(g_hi, wh, preferred_element_type=F32) + jnp.dot(g_hi, wl, preferred_element_type=F32)
         + jnp.dot(g_lo, wh, preferred_element_type=F32) + bg_ref[...])
    gate_chunks(0, n_gate_chunks // 3)
    la = (jnp.minimum(x, 0.0) - jnp.log1p(jnp.exp(-jnp.abs(x)))) * (1.0 / GLA_TAU)
    la_hi, la_lo = split(la)
    inc = inc_ref[...]
    b = jnp.dot(inc, la_hi, preferred_element_type=F32) + jnp.dot(inc, la_lo, preferred_element_type=F32)
    gate_chunks(n_gate_chunks // 3, 2 * n_gate_chunks // 3)

    b3 = b.reshape(n_sub, 8, width)
    la3 = la.reshape(n_sub, 8, width)
    sub = lax.broadcasted_iota(jnp.int32, (1, 8, width), 1)

    def hi_step(y, bit):
        return jnp.where((sub & bit) == 0, pltpu.roll(y, 8 - bit, axis=1), y)

    def lo_step(y, bit):
        return jnp.where((sub & bit) != 0, pltpu.roll(y, bit, axis=1), y)

    def spread(y, groups, pick):
        y4 = y.reshape(n_sub // groups, groups, 8, width)
        return jnp.broadcast_to(y4[:, pick:pick + 1], y4.shape).reshape(n_sub, 8, width)

    hi = {1: b3}
    lo = {1: b3 - la3}
    for bit in (1, 2, 4):
        hi[2 * bit] = hi_step(hi[bit], bit)
        lo[2 * bit] = lo_step(lo[bit], bit)
    for groups in (2, 4, 8):
        hi[8 * groups] = spread(hi[8], groups, groups - 1)
        lo[8 * groups] = spread(lo[8], groups, 0)

    q_all = jnp.concatenate([q_ref[h] for h in range(GLA_HEADS)], axis=1).astype(F32) * (GLA_DK ** -0.5)
    k_all = jnp.concatenate([k_ref[h] for h in range(GLA_HEADS)], axis=1).astype(F32)
    q3 = q_all.reshape(n_sub, 8, width)
    k3 = k_all.reshape(n_sub, 8, width)

    def q_side(m):
        return (q3 * jnp.exp(b3 - lo[m])).reshape(t_rows, width).astype(BF16)

    def k_side(m):
        return (k3 * jnp.exp(hi[m] - b3)).reshape(t_rows, width).astype(BF16)

    qd = {m: q_side(m) for m in GLA_LEVELS + (GLA_CHUNK,)}
    kd = {m: (k_all.astype(BF16) if m == 1 else k_side(m)) for m in GLA_LEVELS + (GLA_CHUNK,)}
    diag = q_all * k_all
    v_heads = [jnp.concatenate([v_ref[2 * h], v_ref[2 * h + 1]], axis=1) for h in range(GLA_HEADS)]
    g_out = gout_ref[...]
    nt = (((1,), (1,)), ((), ()))
    chunks_per_atile = GLA_ATILE // GLA_CHUNK

    for a in range(t_rows // GLA_ATILE):
        if a == 1:
            gate_chunks(2 * n_gate_chunks // 3, n_gate_chunks)
        arows = slice(a * GLA_ATILE, (a + 1) * GLA_ATILE)
        for h in range(GLA_HEADS):
            cols = slice(h * GLA_DK_PAD, (h + 1) * GLA_DK_PAD)
            scores = None
            for li, m in enumerate(GLA_LEVELS):
                s = lax.dot_general(qd[m][arows, cols], kd[m][arows, cols], nt, preferred_element_type=F32)
                s = s * lvl_ref[li]
                scores = s if scores is None else scores + s
            v_a = v_heads[h][arows]
            rd = jnp.sum(diag[arows, cols], axis=-1, keepdims=True)
            intra = jnp.dot(scores.astype(BF16), v_a, preferred_element_type=F32) + rd * v_a.astype(F32)
            for cc in range(chunks_per_atile):
                c = a * chunks_per_atile + cc
                rows = slice(c * GLA_CHUNK, (c + 1) * GLA_CHUNK)
                st = st_ref[h]
                inter = lax.dot_general(qd[GLA_CHUNK][rows, cols], st.astype(BF16), nt,
                                        preferred_element_type=F32)
                o = intra[cc * GLA_CHUNK:(cc + 1) * GLA_CHUNK] + inter
                ms = jnp.sum(o * o, axis=-1, keepdims=True) * (1.0 / GLA_DV)
                o_ref[rows, h * GLA_DV_PAD:(h + 1) * GLA_DV_PAD] = (
                    o * lax.rsqrt(ms + EPS) * g_out).astype(o_ref.dtype)
                last = c * GLA_CHUNK + GLA_CHUNK - 1
                decay = jnp.exp(b[last:last + 1, cols])
                upd = lax.dot_general(v_heads[h][rows], kd[GLA_CHUNK][rows, cols],
                                      (((0,), (0,)), ((), ())), preferred_element_type=F32)
                st_ref[h] = st * decay + upd


def _gla(glaqkv, gl, wup_hi, wup_lo, bg, gout, h, w_gates, l, batch, seq):
    m = gl.shape[0]
    d_model = h.shape[1]
    n_gates = w_gates.shape[2]
    tile = GLA_TILE
    tiles_per_seq = seq // tile
    inc, lvl = _gla_static_matrices(tile)
    row = lambda b, t: b * tiles_per_seq + t
    width = GLA_HEADS * GLA_DK_PAD
    return pl.pallas_call(
        _gla_kernel,
        grid=(batch, tiles_per_seq),
        in_specs=[pl.BlockSpec((GLA_HEADS, tile, 128), lambda b, t: (0, row(b, t), 0)),
                  pl.BlockSpec((GLA_HEADS, tile, 128), lambda b, t: (1, row(b, t), 0)),
                  pl.BlockSpec((2 * GLA_HEADS, tile, 128), lambda b, t: (1, row(b, t), 0)),
                  pl.BlockSpec((tile, 128), lambda b, t: (row(b, t), 0)),
                  pl.BlockSpec((None, 128, width), lambda b, t: (l, 0, 0)),
                  pl.BlockSpec((None, 128, width), lambda b, t: (l, 0, 0)),
                  pl.BlockSpec((None, 1, width), lambda b, t: (l, 0, 0)),
                  pl.BlockSpec((None, 1, GLA_DV_PAD), lambda b, t: (l, 0, 0)),
                  pl.BlockSpec((tile, tile), lambda b, t: (0, 0)),
                  pl.BlockSpec((len(GLA_LEVELS), GLA_ATILE, GLA_ATILE), lambda b, t: (0, 0, 0)),
                  pl.BlockSpec((tile, d_model), lambda b, t: (row(b, t), 0)),
                  pl.BlockSpec((None, d_model, n_gates), lambda b, t: (l, 0, 0),
                               pipeline_mode=pl.Buffered(1))],
        out_specs=[pl.BlockSpec((tile, GLA_HEADS * GLA_DV_PAD), lambda b, t: (row(b, t), 0)),
                   pl.BlockSpec((tile, n_gates), lambda b, t: (row(b, t), 0))],
        out_shape=[jax.ShapeDtypeStruct((m, GLA_HEADS * GLA_DV_PAD), BF16),
                   jax.ShapeDtypeStruct((m, n_gates), BF16)],
        scratch_shapes=[pltpu.VMEM((GLA_HEADS, GLA_DV_PAD, GLA_DK_PAD), F32)],
        compiler_params=_cparams(2),
        name="gla_gates",
    )(glaqkv, glaqkv, glaqkv, gl, wup_hi, wup_lo, bg, gout, inc, lvl, h, w_gates)


def _merge_kernel(oa_ref, ob_ref, oc_ref, ga_ref, gb_ref, gc_ref, pa_ref, pb_ref, pc_ref, y_ref):
    oa = jnp.concatenate([oa_ref[hh] for hh in range(A_HPG)], axis=1)
    ya = jnp.dot(oa, pa_ref[...], preferred_element_type=F32)
    yb = jnp.dot(ob_ref[...], pb_ref[...], preferred_element_type=F32)
    yc = jnp.dot(oc_ref[...], pc_ref[...], preferred_element_type=F32)
    y = (ga_ref[...].astype(F32) * ya + gb_ref[...].astype(F32) * yb + gc_ref[...].astype(F32) * yc)
    y_ref[...] = y.astype(y_ref.dtype)


def _merge(oa, ob, oc, gates, pa, pb, pc, l, tm=256):
    m = ob.shape[0]
    n = pa.shape[2]
    const = lambda i: (l, 0, 0)
    return pl.pallas_call(
        _merge_kernel,
        grid=(m // tm,),
        in_specs=[pl.BlockSpec((A_HPG, tm, A_HEAD_DIM), lambda i: (0, i, 0)),
                  pl.BlockSpec((tm, ob.shape[1]), lambda i: (i, 0)),
                  pl.BlockSpec((tm, oc.shape[1]), lambda i: (i, 0)),
                  pl.BlockSpec((tm, n), lambda i: (i, 0)),
                  pl.BlockSpec((tm, n), lambda i: (i, 1)),
                  pl.BlockSpec((tm, n), lambda i: (i, 2)),
                  pl.BlockSpec((None,) + pa.shape[1:], const),
                  pl.BlockSpec((None,) + pb.shape[1:], const),
                  pl.BlockSpec((None,) + pc.shape[1:], const)],
        out_specs=pl.BlockSpec((tm, n), lambda i: (i, 0)),
        out_shape=jax.ShapeDtypeStruct((m, n), BF16),
        compiler_params=_cparams(1),
        name="merge",
    )(oa, ob, oc, gates, gates, gates, pa, pb, pc)


def _mm_res_kernel(a_ref, w_ref, x_ref, o_ref):
    k = pl.program_id(2)
    acc = jnp.dot(a_ref[...], w_ref[...], preferred_element_type=F32)

    @pl.when(k == 0)
    def _():
        o_ref[...] = x_ref[...] + acc

    @pl.when(k > 0)
    def _():
        o_ref[...] += acc


def _mm_res(a, w, x, l, tm=1024, tn=1024, tk=None):
    m, kdim = a.shape
    n = w.shape[2]
    tk = kdim if tk is None else tk
    return pl.pallas_call(
        _mm_res_kernel,
        grid=(m // tm, n // tn, kdim // tk),
        in_specs=[pl.BlockSpec((tm, tk), lambda i, j, k: (i, k)),
                  pl.BlockSpec((None, tk, tn), lambda i, j, k: (l, k, j)),
                  pl.BlockSpec((tm, tn), lambda i, j, k: (i, j))],
        out_specs=pl.BlockSpec((tm, tn), lambda i, j, k: (i, j)),
        out_shape=jax.ShapeDtypeStruct((m, n), F32),
        compiler_params=_cparams(3),
        name="mm_res",
    )(a, w, x)


def _ffn_in_kernel(x_ref, g_ref, wg_ref, wu_ref, o_ref, h_scr):
    def tile(h):
        g = jnp.dot(h, wg_ref[...], preferred_element_type=F32)
        u = jnp.dot(h, wu_ref[...], preferred_element_type=F32)
        return (g * jax.nn.sigmoid(g) * u).astype(o_ref.dtype)

    @pl.when(pl.program_id(1) == 0)
    def _():
        for c in range(x_ref.shape[0] // NORM_CHUNK):
            rows = slice(c * NORM_CHUNK, (c + 1) * NORM_CHUNK)
            h = _rms(x_ref[rows, :], g_ref[...]).astype(h_scr.dtype)
            h_scr[rows, :] = h
            o_ref[rows, :] = tile(h)

    @pl.when(pl.program_id(1) > 0)
    def _():
        o_ref[...] = tile(h_scr[...])


def _ffn_in(x, g, w, l, tm=1024, tn=512):
    m, d = x.shape
    f = w.shape[2] // 2
    up_off = f // tn
    return pl.pallas_call(
        _ffn_in_kernel,
        grid=(m // tm, f // tn),
        in_specs=[pl.BlockSpec((tm, d), lambda i, j: (i, 0)),
                  pl.BlockSpec((None, 1, d), lambda i, j: (l, 0, 0)),
                  pl.BlockSpec((None, d, tn), lambda i, j: (l, 0, j)),
                  pl.BlockSpec((None, d, tn), lambda i, j: (l, 0, up_off + j))],
        out_specs=pl.BlockSpec((tm, tn), lambda i, j: (i, j)),
        out_shape=jax.ShapeDtypeStruct((m, f), BF16),
        scratch_shapes=[pltpu.VMEM((tm, d), BF16)],
        compiler_params=_cparams(2),
        name="ffn_in",
    )(x, g, w, w)


def _pad_heads(w, off, n_heads, width, padded):
    lead = w.shape[:-1]
    cols = w[..., off:off + n_heads * width].reshape(lead + (n_heads, width))
    cols = jnp.pad(cols, [(0, 0)] * (len(lead) + 1) + [(0, padded - width)])
    return cols.reshape(lead + (n_heads * padded,))


def _prepare_params(w_in, gla_gate_up, gla_gate_b, gla_out_g, sgu_w, sgu_b, w_branch):
    depth = w_in.shape[0]
    w_in_b = w_in.astype(BF16)
    w_attn = w_in_b
    n_gla = _OFF_CU - _OFF_BQ
    w_gla = jnp.pad(w_in_b[:, :, _OFF_BQ:_OFF_CU], ((0, 0), (0, 0), (0, -n_gla % 128)))
    w_sgu = w_in_b[:, :, _OFF_CU:_OFF_GATES]
    w_gates = w_in_b[:, :, _OFF_GATES:]
    wup = _pad_heads(gla_gate_up, 0, GLA_HEADS, GLA_DK, GLA_DK_PAD)
    wup = jnp.pad(wup, ((0, 0), (0, 128 - GLA_RANK), (0, 0)))
    wup_hi = wup.astype(BF16)
    wup_lo = (wup - wup_hi.astype(F32)).astype(BF16)
    bg = _pad_heads(gla_gate_b.reshape(depth, 1, -1), 0, GLA_HEADS, GLA_DK, GLA_DK_PAD)
    gout = jnp.pad(gla_out_g.reshape(depth, 1, GLA_DV), ((0, 0), (0, 0), (0, GLA_DV_PAD - GLA_DV)))
    tril = np.tril(np.ones((SGU_CHUNK, SGU_CHUNK), dtype=bool))
    ws = jnp.where(tril, sgu_w, 0.0).astype(BF16)
    bs_tile = jnp.repeat(jnp.swapaxes(sgu_b, 1, 2), SGU_GW, axis=2)
    pa = w_branch[:, :A_WIDTH].astype(BF16)
    pb = w_branch[:, A_WIDTH:A_WIDTH + GLA_HEADS * GLA_DV].reshape(depth, GLA_HEADS, GLA_DV, D_MODEL)
    pb = jnp.pad(pb, ((0, 0), (0, 0), (0, GLA_DV_PAD - GLA_DV), (0, 0)))
    pb = pb.reshape(depth, GLA_HEADS * GLA_DV_PAD, D_MODEL).astype(BF16)
    pc = w_branch[:, A_WIDTH + GLA_HEADS * GLA_DV:].astype(BF16)
    return dict(w_attn=w_attn, w_gla=w_gla, w_sgu=w_sgu, w_gates=w_gates, wup_hi=wup_hi, wup_lo=wup_lo,
                bg=bg, gout=gout, ws=ws, bs_tile=bs_tile, pa=pa, pb=pb, pc=pc)


def kernel(x, rel_bias, norm1_g, w_in, q_norm_g, k_norm_g, gla_gate_up, gla_gate_b, gla_out_g,
           sgu_ln_g, sgu_ln_b, sgu_w, sgu_b, w_branch, w_out, norm2_g, w_ffn_in, w_ffn_out):
    batch, seq, d = x.shape
    depth = w_in.shape[0]
    m = batch * seq
    assert seq % (A_BLK * max(DILATIONS)) == 0 and seq % GLA_TILE == 0 and d == D_MODEL
    p = _prepare_params(w_in, gla_gate_up, gla_gate_b, gla_out_g, sgu_w, sgu_b, w_branch)
    w_out_b = w_out.astype(BF16)
    w_ffn_in_b = w_ffn_in.astype(BF16)
    w_ffn_out_b = w_ffn_out.astype(BF16)
    n1 = norm1_g.reshape(depth, 1, d)
    n2 = norm2_g.reshape(depth, 1, d)
    gq = q_norm_g.reshape(depth, 1, A_HEAD_DIM)
    gk = k_norm_g.reshape(depth, 1, A_HEAD_DIM)
    ln_g = sgu_ln_g.reshape(depth, 1, SGU_WIDTH)
    ln_b = sgu_ln_b.reshape(depth, 1, SGU_WIDTH)
    bias = _attn_bias_tiles(rel_bias)
    xf = x.reshape(m, d)
    for l in range(depth):
        qkv0, h = _attn_proj_norm(xf, n1, p["w_attn"], gq, gk, l, batch, seq)
        qkvs = [qkv0] + [_attn_proj(h, p["w_attn"], gq, gk, l, gi, batch, seq) for gi in (1, 2)]
        glaqkv, gl = _gla_proj(h, p["w_gla"], l)
        o_c = _sgu(h, p["w_sgu"], ln_g, ln_b, p["ws"], p["bs_tile"], l)
        o_a = _attention(qkvs, bias, batch, seq)
        o_b, gates = _gla(glaqkv, gl, p["wup_hi"], p["wup_lo"], p["bg"], p["gout"], h, p["w_gates"],
                          l, batch, seq)
        y = _merge(o_a, o_b, o_c, gates, p["pa"], p["pb"], p["pc"], l)
        xf = _mm_res(y, w_out_b, xf, l, tm=512, tn=D_MODEL)
        a = _ffn_in(xf, n2, w_ffn_in_b, l)
        xf = _mm_res(a, w_ffn_out_b, xf, l, tk=D_FFN // 2)
    return xf.reshape(batch, seq, d)
```

```python
import functools

import numpy as np
import jax
import jax.numpy as jnp
from jax import lax
from jax.experimental import pallas as pl
from jax.experimental.pallas import tpu as pltpu

F32 = jnp.float32
BF16 = jnp.bfloat16

D_MODEL = 2048
DILATIONS = (1, 4, 16)
N_A_GROUPS = 3
A_HEAD_DIM = 128
A_WIDTH = 512
A_HPG = 4
A_BLK = 128
GLA_HEADS = 4
GLA_DK = 96
GLA_DV = 192
GLA_DK_PAD = 128
GLA_DV_PAD = 256
GLA_RANK = 16
GLA_TAU = 16.0
GLA_CHUNK = 64
GLA_LEVELS = (1, 2, 4, 8, 16, 32)
GLA_ATILE = 128
SGU_WIDTH = 768
SGU_GROUPS = 4
SGU_GW = SGU_WIDTH // SGU_GROUPS
SGU_CHUNK = 128
D_FFN = 5632
REL_BUCKETS = 32
REL_MAX_DIST = 2048
EPS = 1e-6
NEG = -1e30

_OFF_AQ, _OFF_AK, _OFF_AV = 0, 1536, 3072
_OFF_BQ, _OFF_BK, _OFF_BV = 4608, 4992, 5376
_OFF_GL = 6144
_OFF_CU, _OFF_CV = 6160, 6928
_OFF_GATES = 7696

VMEM_LIMIT_BYTES = 56 * 1024 * 1024

GLA_TILE = 256
NORM_CHUNK = 256
GATE_CHUNK = 1024
N_GATE_COLS = 3 * D_MODEL
W_GLA_COLS = 1664
W_SGU_BLK = N_GATE_COLS // (2 * SGU_WIDTH)
W_GLA_BLK = 5


def _cparams(n_axes):
    return pltpu.CompilerParams(
        dimension_semantics=("arbitrary",) * n_axes,
        vmem_limit_bytes=VMEM_LIMIT_BYTES,
    )


def _rms(x, g):
    ms = jnp.mean(x * x, axis=-1, keepdims=True)
    return x * lax.rsqrt(ms + EPS) * g


def _attn_proj_kernel(h_ref, wq_ref, wk_ref, wv_ref, gq_ref, gk_ref, o_ref, *scratch, dil):
    h = h_ref[...]
    sub_rows = h.shape[0] // dil
    for which, (w_ref, g_ref) in enumerate(((wq_ref, gq_ref), (wk_ref, gk_ref), (wv_ref, None))):
        acc = jnp.dot(h, w_ref[...], preferred_element_type=F32)
        for hh in range(A_HPG):
            p = which * A_HPG + hh
            a = acc[:, hh * A_HEAD_DIM:(hh + 1) * A_HEAD_DIM]
            if g_ref is not None:
                a = _rms(a, g_ref[...])
            if dil == 4:
                scr = scratch[0]
                scr[p] = a
                for r in range(dil):
                    o_ref[p, r] = scr[p, pl.ds(r, sub_rows, stride=dil), :].astype(o_ref.dtype)
            else:
                scr, scr2 = scratch
                scr[p] = a
                quarter = h.shape[0] // 4
                for r1 in range(4):
                    scr2[p, r1] = scr[p, pl.ds(r1, quarter, stride=4), :]
                    for r2 in range(4):
                        o_ref[p, r1 + 4 * r2] = scr2[p, r1, pl.ds(r2, sub_rows, stride=4), :].astype(o_ref.dtype)


def _attn_proj(h, w, gq, gk, l, gi, batch, seq, tm=1024):
    d_model = h.shape[1]
    dil = DILATIONS[gi]
    planes = 3 * A_HPG
    tiles_per_seq = seq // tm
    w_spec = lambda which: pl.BlockSpec((None, d_model, A_WIDTH),
                                        lambda b, t: (l, 0, which * N_A_GROUPS + gi))
    return pl.pallas_call(
        functools.partial(_attn_proj_kernel, dil=dil),
        grid=(batch, tiles_per_seq),
        in_specs=[pl.BlockSpec((tm, d_model), lambda b, t: (b * tiles_per_seq + t, 0)),
                  w_spec(0), w_spec(1), w_spec(2),
                  pl.BlockSpec((None, 1, A_HEAD_DIM), lambda b, t: (l, 0, 0)),
                  pl.BlockSpec((None, 1, A_HEAD_DIM), lambda b, t: (l, 0, 0))],
        out_specs=pl.BlockSpec((planes, None, dil, tm // dil, A_HEAD_DIM), lambda b, t: (0, b, 0, t, 0)),
        out_shape=jax.ShapeDtypeStruct((planes, batch, dil, seq // dil, A_HEAD_DIM), BF16),
        scratch_shapes=([pltpu.VMEM((planes, tm, A_HEAD_DIM), F32)]
                        + ([pltpu.VMEM((planes, 4, tm // 4, A_HEAD_DIM), F32)] if dil == 16 else [])),
        compiler_params=_cparams(2),
        name=f"attn_proj_g{gi}",
    )(h, w, w, w, gq, gk)


def _attn_proj_norm_kernel(x_ref, ng_ref, wq_ref, wk_ref, wv_ref, gq_ref, gk_ref, o_ref, h_ref):
    for c in range(x_ref.shape[0] // NORM_CHUNK):
        rows = slice(c * NORM_CHUNK, (c + 1) * NORM_CHUNK)
        h = _rms(x_ref[rows, :], ng_ref[...]).astype(h_ref.dtype)
        h_ref[rows, :] = h
        for which, (w_ref, g_ref) in enumerate(((wq_ref, gq_ref), (wk_ref, gk_ref), (wv_ref, None))):
            acc = jnp.dot(h, w_ref[...], preferred_element_type=F32)
            for hh in range(A_HPG):
                a = acc[:, hh * A_HEAD_DIM:(hh + 1) * A_HEAD_DIM]
                if g_ref is not None:
                    a = _rms(a, g_ref[...])
                o_ref[which * A_HPG + hh, 0, rows, :] = a.astype(o_ref.dtype)


def _attn_proj_norm(x, ng, w, gq, gk, l, batch, seq, tm=1024):
    m, d_model = x.shape
    planes = 3 * A_HPG
    tiles_per_seq = seq // tm
    w_spec = lambda which: pl.BlockSpec((None, d_model, A_WIDTH), lambda b, t: (l, 0, which * N_A_GROUPS))
    row = lambda b, t: b * tiles_per_seq + t
    return pl.pallas_call(
        _attn_proj_norm_kernel,
        grid=(batch, tiles_per_seq),
        in_specs=[pl.BlockSpec((tm, d_model), lambda b, t: (row(b, t), 0)),
                  pl.BlockSpec((None, 1, d_model), lambda b, t: (l, 0, 0)),
                  w_spec(0), w_spec(1), w_spec(2),
                  pl.BlockSpec((None, 1, A_HEAD_DIM), lambda b, t: (l, 0, 0)),
                  pl.BlockSpec((None, 1, A_HEAD_DIM), lambda b, t: (l, 0, 0))],
        out_specs=[pl.BlockSpec((planes, None, 1, tm, A_HEAD_DIM), lambda b, t: (0, b, 0, t, 0)),
                   pl.BlockSpec((tm, d_model), lambda b, t: (row(b, t), 0))],
        out_shape=[jax.ShapeDtypeStruct((planes, batch, 1, seq, A_HEAD_DIM), BF16),
                   jax.ShapeDtypeStruct((m, d_model), BF16)],
        compiler_params=_cparams(2),
        name="attn_proj_norm_g0",
    )(x, ng, w, w, w, gq, gk)


def _gla_proj_kernel(h_ref, w_ref, o_ref, gl_ref):
    acc = jnp.dot(h_ref[...], w_ref[...], preferred_element_type=F32)
    rows = acc.shape[0]

    def plane(off, width):
        a = acc[:, off:off + width]
        if width < 128:
            a = jnp.concatenate([a, jnp.zeros((rows, 128 - width), F32)], axis=1)
        return a

    for hh in range(GLA_HEADS):
        o_ref[hh] = plane(hh * GLA_DK, GLA_DK).astype(o_ref.dtype)
        o_ref[GLA_HEADS + hh] = plane(GLA_HEADS * GLA_DK + hh * GLA_DK, GLA_DK).astype(o_ref.dtype)
        v_off = 2 * GLA_HEADS * GLA_DK + hh * GLA_DV
        o_ref[2 * GLA_HEADS + 2 * hh] = plane(v_off, 128).astype(o_ref.dtype)
        o_ref[2 * GLA_HEADS + 2 * hh + 1] = plane(v_off + 128, GLA_DV - 128).astype(o_ref.dtype)
    gl_ref[...] = plane(2 * GLA_HEADS * GLA_DK + GLA_HEADS * GLA_DV, GLA_RANK)


def _gla_proj(h, w, l, tm=512):
    m, d = h.shape
    n = W_GLA_COLS
    n_blk = 4 * GLA_HEADS
    return pl.pallas_call(
        _gla_proj_kernel,
        grid=(m // tm,),
        in_specs=[pl.BlockSpec((tm, d), lambda i: (i, 0)),
                  pl.BlockSpec((None, d, n), lambda i: (l, 0, W_GLA_BLK))],
        out_specs=[pl.BlockSpec((n_blk, tm, 128), lambda i: (0, i, 0)),
                   pl.BlockSpec((tm, 128), lambda i: (i, 0))],
        out_shape=[jax.ShapeDtypeStruct((n_blk, m, 128), BF16),
                   jax.ShapeDtypeStruct((m, 128), F32)],
        compiler_params=_cparams(1),
        name="gla_proj",
    )(h, w)


def _sgu_kernel(h_ref, w_ref, lng_ref, lnb_ref, ws_ref, bs_ref, o_ref):
    acc = jnp.dot(h_ref[...], w_ref[...], preferred_element_type=F32)
    u = jax.nn.gelu(acc[:, :SGU_WIDTH])
    gv = jax.nn.gelu(acc[:, SGU_WIDTH:])
    mu = jnp.mean(gv, axis=-1, keepdims=True)
    xc = gv - mu
    var = jnp.mean(xc * xc, axis=-1, keepdims=True)
    vh = (xc * lax.rsqrt(var + EPS) * lng_ref[...] + lnb_ref[...]).astype(BF16)
    half = 2 * SGU_GW
    lane = lax.broadcasted_iota(jnp.int32, (SGU_CHUNK, half), 1)
    first = lane < SGU_GW
    for c in range(h_ref.shape[0] // SGU_CHUNK):
        rows = slice(c * SGU_CHUNK, (c + 1) * SGU_CHUNK)
        for p in range(2):
            cols = slice(p * half, (p + 1) * half)
            vc = vh[rows, cols]
            r0 = jnp.dot(ws_ref[2 * p], vc, preferred_element_type=F32)
            r1 = jnp.dot(ws_ref[2 * p + 1], vc, preferred_element_type=F32)
            f = jnp.where(first, r0, r1) + bs_ref[:, cols]
            o_ref[rows, cols] = (u[rows, cols] * f).astype(o_ref.dtype)


def _sgu(h, w, ln_g, ln_b, ws, bs_tile, l, tm=512):
    m, d = h.shape
    n = 2 * SGU_WIDTH
    return pl.pallas_call(
        _sgu_kernel,
        grid=(m // tm,),
        in_specs=[pl.BlockSpec((tm, d), lambda i: (i, 0)),
                  pl.BlockSpec((None, d, n), lambda i: (l, 0, W_SGU_BLK)),
                  pl.BlockSpec((None, 1, SGU_WIDTH), lambda i: (l, 0, 0)),
                  pl.BlockSpec((None, 1, SGU_WIDTH), lambda i: (l, 0, 0)),
                  pl.BlockSpec((None, SGU_GROUPS, SGU_CHUNK, SGU_CHUNK), lambda i: (l, 0, 0, 0)),
                  pl.BlockSpec((None, SGU_CHUNK, SGU_WIDTH), lambda i: (l, 0, 0))],
        out_specs=pl.BlockSpec((tm, SGU_WIDTH), lambda i: (i, 0)),
        out_shape=jax.ShapeDtypeStruct((m, SGU_WIDTH), BF16),
        compiler_params=_cparams(1),
        name="sgu",
    )(h, w, ln_g, ln_b, ws, bs_tile)


def _t5_causal_bucket(dist):
    max_exact = REL_BUCKETS // 2
    d = np.maximum(dist, 1)
    large = max_exact + (np.log(d / max_exact) / np.log(REL_MAX_DIST / max_exact)
                         * (REL_BUCKETS - max_exact)).astype(np.int64)
    large = np.minimum(large, REL_BUCKETS - 1)
    return np.where(dist < max_exact, dist, large).astype(np.int32)


def _attn_bias_tiles(rel_bias):
    dist = np.arange(A_BLK + 1)
    period = 2 * A_BLK + 1
    tiles = []
    for gi, dilation in enumerate(DILATIONS):
        onehot = np.eye(REL_BUCKETS, dtype=np.float32)[_t5_causal_bucket(dist * dilation)]
        heads = rel_bias[:, gi * A_HPG:(gi + 1) * A_HPG].astype(F32)
        vec = jnp.dot(jnp.asarray(onehot), heads, precision=lax.Precision.HIGHEST).T
        seq = jnp.concatenate([vec[:, ::-1], jnp.full((A_HPG, period - A_BLK - 1), NEG, F32)], axis=1)
        flat = jnp.tile(seq, (1, A_BLK))[:, :A_BLK * 2 * A_BLK]
        tiles.append(flat.reshape(A_HPG, A_BLK, 2 * A_BLK))
    return jnp.stack(tiles, axis=0)


def _attn_unit(q, k, v, bias, scale):
    s = lax.dot_general(q, k, (((1,), (1,)), ((), ())), preferred_element_type=F32) * scale + bias
    mx = jnp.max(s, axis=-1, keepdims=True)
    p = jnp.exp(s - mx)
    den = jnp.sum(p, axis=-1, keepdims=True)
    o = jnp.dot(p.astype(BF16), v, preferred_element_type=F32) / den
    return o, mx + jnp.log(den)


def _attn_kernel(q0, k0, v0, q1, k1, v1, q2, k2, v2, bias_ref, o_ref, og, lg):
    scale = A_HEAD_DIM ** -0.5
    for gi, (q_ref, k_ref, v_ref) in enumerate(((q0, k0, v0), (q1, k1, v1), (q2, k2, v2))):
        dil = DILATIONS[gi]
        n_blk = q_ref.shape[1] // A_BLK
        bias = bias_ref[gi]
        for r in range(dil):
            for n in range(n_blk):
                q = q_ref[r, n * A_BLK:(n + 1) * A_BLK, :]
                if n == 0:
                    k = k_ref[r, 0:A_BLK, :]
                    v = v_ref[r, 0:A_BLK, :]
                    o, lse = _attn_unit(q, k, v, bias[:, A_BLK:], scale)
                else:
                    k = k_ref[r, (n - 1) * A_BLK:(n + 1) * A_BLK, :]
                    v = v_ref[r, (n - 1) * A_BLK:(n + 1) * A_BLK, :]
                    o, lse = _attn_unit(q, k, v, bias, scale)
                start = r + dil * A_BLK * n
                rows = pl.ds(start, A_BLK) if dil == 1 else pl.ds(start, A_BLK, stride=dil)
                og[gi, rows, :] = o
                lg[gi, rows, :] = jnp.broadcast_to(lse, (A_BLK, A_HEAD_DIM))
    l0, l1, l2 = lg[0], lg[1], lg[2]
    mx = jnp.maximum(jnp.maximum(l0, l1), l2)
    w0, w1, w2 = jnp.exp(l0 - mx), jnp.exp(l1 - mx), jnp.exp(l2 - mx)
    o = (w0 * og[0] + w1 * og[1] + w2 * og[2]) / (w0 + w1 + w2)
    o_ref[...] = o.astype(o_ref.dtype)


def _attention(qkvs, bias, batch, seq):
    in_specs = []
    args = []
    for gi, qkv in enumerate(qkvs):
        dil = DILATIONS[gi]
        for which in range(3):
            in_specs.append(pl.BlockSpec((None, None, dil, seq // dil, A_HEAD_DIM),
                                         lambda b, hh, which=which: (which * A_HPG + hh, b, 0, 0, 0)))
            args.append(qkv)
    in_specs.append(pl.BlockSpec((N_A_GROUPS, None, A_BLK, 2 * A_BLK), lambda b, hh: (0, hh, 0, 0)))
    return pl.pallas_call(
        _attn_kernel,
        grid=(batch, A_HPG),
        in_specs=in_specs,
        out_specs=pl.BlockSpec((None, seq, A_HEAD_DIM), lambda b, hh: (hh, b, 0)),
        out_shape=jax.ShapeDtypeStruct((A_HPG, batch * seq, A_HEAD_DIM), BF16),
        scratch_shapes=[pltpu.VMEM((N_A_GROUPS, seq, A_HEAD_DIM), F32),
                        pltpu.VMEM((N_A_GROUPS, seq, A_HEAD_DIM), F32)],
        compiler_params=_cparams(2),
        name="attention",
    )(*args, bias)


def _gla_static_matrices(tile):
    t = np.arange(tile)
    same = (t[:, None] // GLA_CHUNK) == (t[None, :] // GLA_CHUNK)
    inc = same & (t[None, :] <= t[:, None])
    a = np.arange(GLA_ATILE)
    ti, si = a[:, None], a[None, :]
    masks = []
    for m in GLA_LEVELS:
        masks.append((ti // (2 * m) == si // (2 * m)) & ((ti // m) % 2 == 1) & ((si // m) % 2 == 0))
    return jnp.asarray(inc, BF16), jnp.asarray(np.stack(masks), F32)


def _gla_kernel(q_ref, k_ref, v_ref, gl_ref, wh_ref, wl_ref, bg_ref, gout_ref, inc_ref, lvl_ref,
                h_ref, wg_ref, o_ref, gates_ref, st_ref):
    @pl.when(pl.program_id(1) == 0)
    def _():
        st_ref[...] = jnp.zeros_like(st_ref)

    n_gate_chunks = gates_ref.shape[1] // GATE_CHUNK

    def gate_chunks(first, last):
        for c in range(first, last):
            cols = slice(c * GATE_CHUNK, (c + 1) * GATE_CHUNK)
            acc = jnp.dot(h_ref[...], wg_ref[:, cols], preferred_element_type=F32)
            gates_ref[:, cols] = jax.nn.sigmoid(acc).astype(gates_ref.dtype)

    t_rows = gl_ref.shape[0]
    n_sub = t_rows // 8
    width = GLA_HEADS * GLA_DK_PAD

    def split(x):
        hi = x.astype(BF16)
        return hi, (x - hi.astype(F32)).astype(BF16)

    g_hi, g_lo = split(gl_ref[...])
    wh, wl = wh_ref[...], wl_ref[...]
    x = (jnp.dot(g_hi, wh, preferred_element_type=F32) + jnp.dot(g_hi, wl, preferred_element_type=F32)
         + jnp.dot(g_lo, wh, preferred_element_type=F32) + bg_ref[...])
    gate_chunks(0, n_gate_chunks // 3)
    la = (jnp.minimum(x, 0.0) - jnp.log1p(jnp.exp(-jnp.abs(x)))) * (1.0 / GLA_TAU)
    la_hi, la_lo = split(la)
    inc = inc_ref[...]
    b = jnp.dot(inc, la_hi, preferred_element_type=F32) + jnp.dot(inc, la_lo, preferred_element_type=F32)
    gate_chunks(n_gate_chunks // 3, 2 * n_gate_chunks // 3)

    b3 = b.reshape(n_sub, 8, width)
    la3 = la.reshape(n_sub, 8, width)
    sub = lax.broadcasted_iota(jnp.int32, (1, 8, width), 1)

    def hi_step(y, bit):
        return jnp.where((sub & bit) == 0, pltpu.roll(y, 8 - bit, axis=1), y)

    def lo_step(y, bit):
        return jnp.where((sub & bit) != 0, pltpu.roll(y, bit, axis=1), y)

    def spread(y, groups, pick):
        y4 = y.reshape(n_sub // groups, groups, 8, width)
        return jnp.broadcast_to(y4[:, pick:pick + 1], y4.shape).reshape(n_sub, 8, width)

    hi = {1: b3}
    lo = {1: b3 - la3}
    for bit in (1, 2, 4):
        hi[2 * bit] = hi_step(hi[bit], bit)
        lo[2 * bit] = lo_step(lo[bit], bit)
    for groups in (2, 4, 8):
        hi[8 * groups] = spread(hi[8], groups, groups - 1)
        lo[8 * groups] = spread(lo[8], groups, 0)

    q_all = jnp.concatenate([q_ref[h] for h in range(GLA_HEADS)], axis=1).astype(F32) * (GLA_DK ** -0.5)
    k_all = jnp.concatenate([k_ref[h] for h in range(GLA_HEADS)], axis=1).astype(F32)
    q3 = q_all.reshape(n_sub, 8, width)
    k3 = k_all.reshape(n_sub, 8, width)

    def q_side(m):
        return (q3 * jnp.exp(b3 - lo[m])).reshape(t_rows, width).astype(BF16)

    def k_side(m):
        return (k3 * jnp.exp(hi[m] - b3)).reshape(t_rows, width).astype(BF16)

    qd = {m: q_side(m) for m in GLA_LEVELS + (GLA_CHUNK,)}
    kd = {m: (k_all.astype(BF16) if m == 1 else k_side(m)) for m in GLA_LEVELS + (GLA_CHUNK,)}
    diag = q_all * k_all
    v_heads = [jnp.concatenate([v_ref[2 * h], v_ref[2 * h + 1]], axis=1) for h in range(GLA_HEADS)]
    g_out = gout_ref[...]
    nt = (((1,), (1,)), ((), ()))
    chunks_per_atile = GLA_ATILE // GLA_CHUNK

    for a in range(t_rows // GLA_ATILE):
        if a == 1:
            gate_chunks(2 * n_gate_chunks // 3, n_gate_chunks)
        arows = slice(a * GLA_ATILE, (a + 1) * GLA_ATILE)
        for h in range(GLA_HEADS):
            cols = slice(h * GLA_DK_PAD, (h + 1) * GLA_DK_PAD)
            scores = None
            for li, m in enumerate(GLA_LEVELS):
                s = lax.dot_general(qd[m][arows, cols], kd[m][arows, cols], nt, preferred_element_type=F32)
                s = s * lvl_ref[li]
                scores = s if scores is None else scores + s
            v_a = v_heads[h][arows]
            rd = jnp.sum(diag[arows, cols], axis=-1, keepdims=True)
            intra = jnp.dot(scores.astype(BF16), v_a, preferred_element_type=F32) + rd * v_a.astype(F32)
            for cc in range(chunks_per_atile):
                c = a * chunks_per_atile + cc
                rows = slice(c * GLA_CHUNK, (c + 1) * GLA_CHUNK)
                st = st_ref[h]
                inter = lax.dot_general(qd[GLA_CHUNK][rows, cols], st.astype(BF16), nt,
                                        preferred_element_type=F32)
                o = intra[cc * GLA_CHUNK:(cc + 1) * GLA_CHUNK] + inter
                ms = jnp.sum(o * o, axis=-1, keepdims=True) * (1.0 / GLA_DV)
                o_n = (o * lax.rsqrt(ms + EPS) * g_out).astype(o_ref.dtype)
                o_ref[rows, h * GLA_DV:(h + 1) * GLA_DV] = o_n[:, :GLA_DV]
                last = c * GLA_CHUNK + GLA_CHUNK - 1
                decay = jnp.exp(b[last:last + 1, cols])
                upd = lax.dot_general(v_heads[h][rows], kd[GLA_CHUNK][rows, cols],
                                      (((0,), (0,)), ((), ())), preferred_element_type=F32)
                st_ref[h] = st * decay + upd


def _gla(glaqkv, gl, wup_hi, wup_lo, bg, gout, h, w_gates, l, batch, seq):
    m = gl.shape[0]
    d_model = h.shape[1]
    n_gates = N_GATE_COLS
    tile = GLA_TILE
    tiles_per_seq = seq // tile
    inc, lvl = _gla_static_matrices(tile)
    row = lambda b, t: b * tiles_per_seq + t
    width = GLA_HEADS * GLA_DK_PAD
    return pl.pallas_call(
        _gla_kernel,
        grid=(batch, tiles_per_seq),
        in_specs=[pl.BlockSpec((GLA_HEADS, tile, 128), lambda b, t: (0, row(b, t), 0)),
                  pl.BlockSpec((GLA_HEADS, tile, 128), lambda b, t: (1, row(b, t), 0)),
                  pl.BlockSpec((2 * GLA_HEADS, tile, 128), lambda b, t: (1, row(b, t), 0)),
                  pl.BlockSpec((tile, 128), lambda b, t: (row(b, t), 0)),
                  pl.BlockSpec((None, 128, width), lambda b, t: (l, 0, 0)),
                  pl.BlockSpec((None, 128, width), lambda b, t: (l, 0, 0)),
                  pl.BlockSpec((None, 1, width), lambda b, t: (l, 0, 0)),
                  pl.BlockSpec((None, 1, GLA_DV_PAD), lambda b, t: (l, 0, 0)),
                  pl.BlockSpec((tile, tile), lambda b, t: (0, 0)),
                  pl.BlockSpec((len(GLA_LEVELS), GLA_ATILE, GLA_ATILE), lambda b, t: (0, 0, 0)),
                  pl.BlockSpec((tile, d_model), lambda b, t: (row(b, t), 0)),
                  pl.BlockSpec((None, d_model, n_gates), lambda b, t: (l, 0, 0),
                               pipeline_mode=pl.Buffered(1))],
        out_specs=[pl.BlockSpec((tile, GLA_HEADS * GLA_DV), lambda b, t: (row(b, t), 0)),
                   pl.BlockSpec((tile, n_gates), lambda b, t: (row(b, t), 0))],
        out_shape=[jax.ShapeDtypeStruct((m, GLA_HEADS * GLA_DV), BF16),
                   jax.ShapeDtypeStruct((m, n_gates), BF16)],
        scratch_shapes=[pltpu.VMEM((GLA_HEADS, GLA_DV_PAD, GLA_DK_PAD), F32)],
        compiler_params=_cparams(2),
        name="gla_gates",
    )(glaqkv, glaqkv, glaqkv, gl, wup_hi, wup_lo, bg, gout, inc, lvl, h, w_gates)


def _merge_kernel(oa_ref, ob_ref, oc_ref, ga_ref, gb_ref, gc_ref, pa_ref, pb_ref, pc_ref, y_ref):
    oa = jnp.concatenate([oa_ref[hh] for hh in range(A_HPG)], axis=1)
    ya = jnp.dot(oa, pa_ref[...], preferred_element_type=F32)
    yb = jnp.dot(ob_ref[...], pb_ref[...], preferred_element_type=F32)
    yc = jnp.dot(oc_ref[...], pc_ref[...], preferred_element_type=F32)
    y = (ga_ref[...].astype(F32) * ya + gb_ref[...].astype(F32) * yb + gc_ref[...].astype(F32) * yc)
    y_ref[...] = y.astype(y_ref.dtype)


def _merge(oa, ob, oc, gates, pa, pb, pc, l, tm=256):
    m = ob.shape[0]
    n = pa.shape[2]
    const = lambda i: (l, 0, 0)
    return pl.pallas_call(
        _merge_kernel,
        grid=(m // tm,),
        in_specs=[pl.BlockSpec((A_HPG, tm, A_HEAD_DIM), lambda i: (0, i, 0)),
                  pl.BlockSpec((tm, ob.shape[1]), lambda i: (i, 0)),
                  pl.BlockSpec((tm, oc.shape[1]), lambda i: (i, 0)),
                  pl.BlockSpec((tm, n), lambda i: (i, 0)),
                  pl.BlockSpec((tm, n), lambda i: (i, 1)),
                  pl.BlockSpec((tm, n), lambda i: (i, 2)),
                  pl.BlockSpec((None,) + pa.shape[1:], const),
                  pl.BlockSpec((None,) + pb.shape[1:], const),
                  pl.BlockSpec((None,) + pc.shape[1:], const)],
        out_specs=pl.BlockSpec((tm, n), lambda i: (i, 0)),
        out_shape=jax.ShapeDtypeStruct((m, n), BF16),
        compiler_params=_cparams(1),
        name="merge",
    )(oa, ob, oc, gates, gates, gates, pa, pb, pc)


def _mm_res_kernel(a_ref, w_ref, x_ref, o_ref):
    k = pl.program_id(2)
    acc = jnp.dot(a_ref[...], w_ref[...], preferred_element_type=F32)

    @pl.when(k == 0)
    def _():
        o_ref[...] = x_ref[...] + acc

    @pl.when(k > 0)
    def _():
        o_ref[...] += acc


def _mm_res(a, w, x, l, tm=1024, tn=1024, tk=None):
    m, kdim = a.shape
    n = w.shape[2]
    tk = kdim if tk is None else tk
    return pl.pallas_call(
        _mm_res_kernel,
        grid=(m // tm, n // tn, kdim // tk),
        in_specs=[pl.BlockSpec((tm, tk), lambda i, j, k: (i, k)),
                  pl.BlockSpec((None, tk, tn), lambda i, j, k: (l, k, j)),
                  pl.BlockSpec((tm, tn), lambda i, j, k: (i, j))],
        out_specs=pl.BlockSpec((tm, tn), lambda i, j, k: (i, j)),
        out_shape=jax.ShapeDtypeStruct((m, n), F32),
        compiler_params=_cparams(3),
        name="mm_res",
    )(a, w, x)


def _ffn_in_kernel(x_ref, g_ref, wg_ref, wu_ref, o_ref, h_scr):
    def tile(h):
        g = jnp.dot(h, wg_ref[...], preferred_element_type=F32)
        u = jnp.dot(h, wu_ref[...], preferred_element_type=F32)
        return (g * jax.nn.sigmoid(g) * u).astype(o_ref.dtype)

    @pl.when(pl.program_id(1) == 0)
    def _():
        for c in range(x_ref.shape[0] // NORM_CHUNK):
            rows = slice(c * NORM_CHUNK, (c + 1) * NORM_CHUNK)
            h = _rms(x_ref[rows, :], g_ref[...]).astype(h_scr.dtype)
            h_scr[rows, :] = h
            o_ref[rows, :] = tile(h)

    @pl.when(pl.program_id(1) > 0)
    def _():
        o_ref[...] = tile(h_scr[...])


def _ffn_in(x, g, w, l, tm=1024, tn=512):
    m, d = x.shape
    f = w.shape[2] // 2
    up_off = f // tn
    return pl.pallas_call(
        _ffn_in_kernel,
        grid=(m // tm, f // tn),
        in_specs=[pl.BlockSpec((tm, d), lambda i, j: (i, 0)),
                  pl.BlockSpec((None, 1, d), lambda i, j: (l, 0, 0)),
                  pl.BlockSpec((None, d, tn), lambda i, j: (l, 0, j)),
                  pl.BlockSpec((None, d, tn), lambda i, j: (l, 0, up_off + j))],
        out_specs=pl.BlockSpec((tm, tn), lambda i, j: (i, j)),
        out_shape=jax.ShapeDtypeStruct((m, f), BF16),
        scratch_shapes=[pltpu.VMEM((tm, d), BF16)],
        compiler_params=_cparams(2),
        name="ffn_in",
    )(x, g, w, w)


def _pad_heads(w, off, n_heads, width, padded):
    lead = w.shape[:-1]
    cols = w[..., off:off + n_heads * width].reshape(lead + (n_heads, width))
    cols = jnp.pad(cols, [(0, 0)] * (len(lead) + 1) + [(0, padded - width)])
    return cols.reshape(lead + (n_heads * padded,))


def _prepare_params(w_in, gla_gate_up, gla_gate_b, gla_out_g, sgu_w, sgu_b, w_branch):
    depth = w_in.shape[0]
    w_attn = w_in[:, :, _OFF_AQ:_OFF_BQ].astype(BF16)
    zeros = lambda n: jnp.zeros(w_in.shape[:2] + (n,), w_in.dtype)
    n_gla = _OFF_CU - _OFF_BQ
    w_rest = jnp.concatenate([
        w_in[:, :, _OFF_GATES:], w_in[:, :, _OFF_CU:_OFF_GATES],
        zeros(W_GLA_BLK * W_GLA_COLS - N_GATE_COLS - 2 * SGU_WIDTH),
        w_in[:, :, _OFF_BQ:_OFF_CU], zeros(W_GLA_COLS - n_gla)], axis=-1).astype(BF16)
    w_gla = w_sgu = w_gates = w_rest
    wup = _pad_heads(gla_gate_up, 0, GLA_HEADS, GLA_DK, GLA_DK_PAD)
    wup = jnp.pad(wup, ((0, 0), (0, 128 - GLA_RANK), (0, 0)))
    wup_hi = wup.astype(BF16)
    wup_lo = (wup - wup_hi.astype(F32)).astype(BF16)
    bg = _pad_heads(gla_gate_b.reshape(depth, 1, -1), 0, GLA_HEADS, GLA_DK, GLA_DK_PAD)
    gout = jnp.pad(gla_out_g.reshape(depth, 1, GLA_DV), ((0, 0), (0, 0), (0, GLA_DV_PAD - GLA_DV)))
    tril = np.tril(np.ones((SGU_CHUNK, SGU_CHUNK), dtype=bool))
    ws = jnp.where(tril, sgu_w, 0.0).astype(BF16)
    bs_tile = jnp.repeat(jnp.swapaxes(sgu_b, 1, 2), SGU_GW, axis=2)
    pa = w_branch[:, :A_WIDTH].astype(BF16)
    pb = w_branch[:, A_WIDTH:A_WIDTH + GLA_HEADS * GLA_DV].astype(BF16)
    pc = w_branch[:, A_WIDTH + GLA_HEADS * GLA_DV:].astype(BF16)
    return dict(w_attn=w_attn, w_gla=w_gla, w_sgu=w_sgu, w_gates=w_gates, wup_hi=wup_hi, wup_lo=wup_lo,
                bg=bg, gout=gout, ws=ws, bs_tile=bs_tile, pa=pa, pb=pb, pc=pc)


def kernel(x, rel_bias, norm1_g, w_in, q_norm_g, k_norm_g, gla_gate_up, gla_gate_b, gla_out_g,
           sgu_ln_g, sgu_ln_b, sgu_w, sgu_b, w_branch, w_out, norm2_g, w_ffn_in, w_ffn_out):
    batch, seq, d = x.shape
    depth = w_in.shape[0]
    m = batch * seq
    assert seq % (A_BLK * max(DILATIONS)) == 0 and seq % GLA_TILE == 0 and d == D_MODEL
    p = _prepare_params(w_in, gla_gate_up, gla_gate_b, gla_out_g, sgu_w, sgu_b, w_branch)
    w_out_b = w_out.astype(BF16)
    w_ffn_in_b = w_ffn_in.astype(BF16)
    w_ffn_out_b = w_ffn_out.astype(BF16)
    n1 = norm1_g.reshape(depth, 1, d)
    n2 = norm2_g.reshape(depth, 1, d)
    gq = q_norm_g.reshape(depth, 1, A_HEAD_DIM)
    gk = k_norm_g.reshape(depth, 1, A_HEAD_DIM)
    ln_g = sgu_ln_g.reshape(depth, 1, SGU_WIDTH)
    ln_b = sgu_ln_b.reshape(depth, 1, SGU_WIDTH)
    bias = _attn_bias_tiles(rel_bias)
    xf = x.reshape(m, d)
    for l in range(depth):
        qkv0, h = _attn_proj_norm(xf, n1, p["w_attn"], gq, gk, l, batch, seq)
        qkvs = [qkv0] + [_attn_proj(h, p["w_attn"], gq, gk, l, gi, batch, seq) for gi in (1, 2)]
        glaqkv, gl = _gla_proj(h, p["w_gla"], l)
        o_c = _sgu(h, p["w_sgu"], ln_g, ln_b, p["ws"], p["bs_tile"], l)
        o_a = _attention(qkvs, bias, batch, seq)
        o_b, gates = _gla(glaqkv, gl, p["wup_hi"], p["wup_lo"], p["bg"], p["gout"], h, p["w_gates"],
                          l, batch, seq)
        y = _merge(o_a, o_b, o_c, gates, p["pa"], p["pb"], p["pc"], l)
        xf = _mm_res(y, w_out_b, xf, l, tm=512, tn=D_MODEL)
        a = _ffn_in(xf, n2, w_ffn_in_b, l)
        xf = _mm_res(a, w_ffn_out_b, xf, l, tm=1024, tn=512)
    return xf.reshape(batch, seq, d)
```

```python
import functools

import numpy as np
import jax
import jax.numpy as jnp
from jax import lax
from jax.experimental import pallas as pl
from jax.experimental.pallas import tpu as pltpu

F32 = jnp.float32
BF16 = jnp.bfloat16

D_MODEL = 2048
DILATIONS = (1, 4, 16)
N_A_GROUPS = 3
A_HEAD_DIM = 128
A_WIDTH = 512
A_HPG = 4
A_BLK = 128
GLA_HEADS = 4
GLA_DK = 96
GLA_DV = 192
GLA_DK_PAD = 128
GLA_DV_PAD = 256
GLA_RANK = 16
GLA_TAU = 16.0
GLA_CHUNK = 64
GLA_LEVELS = (1, 2, 4, 8, 16, 32)
GLA_ATILE = 128
SGU_WIDTH = 768
SGU_GROUPS = 4
SGU_GW = SGU_WIDTH // SGU_GROUPS
SGU_CHUNK = 128
D_FFN = 5632
REL_BUCKETS = 32
REL_MAX_DIST = 2048
EPS = 1e-6
NEG = -1e30

_OFF_AQ, _OFF_AK, _OFF_AV = 0, 1536, 3072
_OFF_BQ, _OFF_BK, _OFF_BV = 4608, 4992, 5376
_OFF_GL = 6144
_OFF_CU, _OFF_CV = 6160, 6928
_OFF_GATES = 7696

VMEM_LIMIT_BYTES = 56 * 1024 * 1024

GLA_TILE = 256
NORM_CHUNK = 256
GATE_CHUNK = 1024
ATTN_SPLIT = 2
N_GATE_COLS = 3 * D_MODEL
W_GLA_COLS = 1664


def _cparams(n_axes):
    return pltpu.CompilerParams(
        dimension_semantics=("arbitrary",) * n_axes,
        vmem_limit_bytes=VMEM_LIMIT_BYTES,
    )


def _rms(x, g):
    ms = jnp.mean(x * x, axis=-1, keepdims=True)
    return x * lax.rsqrt(ms + EPS) * g


def _attn_proj_kernel(h_ref, wq_ref, wk_ref, wv_ref, gq_ref, gk_ref, o_ref, *scratch, dil):
    h = h_ref[...]
    sub_rows = h.shape[0] // dil
    for which, (w_ref, g_ref) in enumerate(((wq_ref, gq_ref), (wk_ref, gk_ref), (wv_ref, None))):
        acc = jnp.dot(h, w_ref[...], preferred_element_type=F32)
        for hh in range(A_HPG):
            p = which * A_HPG + hh
            a = acc[:, hh * A_HEAD_DIM:(hh + 1) * A_HEAD_DIM]
            if g_ref is not None:
                a = _rms(a, g_ref[...])
            if dil == 4:
                scr = scratch[0]
                scr[p] = a
                for r in range(dil):
                    o_ref[p, r] = scr[p, pl.ds(r, sub_rows, stride=dil), :].astype(o_ref.dtype)
            else:
                scr, scr2 = scratch
                scr[p] = a
                quarter = h.shape[0] // 4
                for r1 in range(4):
                    scr2[p, r1] = scr[p, pl.ds(r1, quarter, stride=4), :]
                    for r2 in range(4):
                        o_ref[p, r1 + 4 * r2] = scr2[p, r1, pl.ds(r2, sub_rows, stride=4), :].astype(o_ref.dtype)


def _attn_proj(h, w, gq, gk, l, gi, batch, seq, tm=1024):
    d_model = h.shape[1]
    dil = DILATIONS[gi]
    planes = 3 * A_HPG
    tiles_per_seq = seq // tm
    w_spec = lambda which: pl.BlockSpec((None, d_model, A_WIDTH),
                                        lambda b, t: (l, 0, which * N_A_GROUPS + gi))
    return pl.pallas_call(
        functools.partial(_attn_proj_kernel, dil=dil),
        grid=(batch, tiles_per_seq),
        in_specs=[pl.BlockSpec((tm, d_model), lambda b, t: (b * tiles_per_seq + t, 0)),
                  w_spec(0), w_spec(1), w_spec(2),
                  pl.BlockSpec((None, 1, A_HEAD_DIM), lambda b, t: (l, 0, 0)),
                  pl.BlockSpec((None, 1, A_HEAD_DIM), lambda b, t: (l, 0, 0))],
        out_specs=pl.BlockSpec((planes, None, dil, tm // dil, A_HEAD_DIM), lambda b, t: (0, b, 0, t, 0)),
        out_shape=jax.ShapeDtypeStruct((planes, batch, dil, seq // dil, A_HEAD_DIM), BF16),
        scratch_shapes=([pltpu.VMEM((planes, tm, A_HEAD_DIM), F32)]
                        + ([pltpu.VMEM((planes, 4, tm // 4, A_HEAD_DIM), F32)] if dil == 16 else [])),
        compiler_params=_cparams(2),
        name=f"attn_proj_g{gi}",
    )(h, w, w, w, gq, gk)


def _attn_proj_norm_kernel(x_ref, ng_ref, wq_ref, wk_ref, wv_ref, gq_ref, gk_ref, o_ref, h_ref):
    for c in range(x_ref.shape[0] // NORM_CHUNK):
        rows = slice(c * NORM_CHUNK, (c + 1) * NORM_CHUNK)
        h = _rms(x_ref[rows, :], ng_ref[...]).astype(h_ref.dtype)
        h_ref[rows, :] = h
        for which, (w_ref, g_ref) in enumerate(((wq_ref, gq_ref), (wk_ref, gk_ref), (wv_ref, None))):
            acc = jnp.dot(h, w_ref[...], preferred_element_type=F32)
            for hh in range(A_HPG):
                a = acc[:, hh * A_HEAD_DIM:(hh + 1) * A_HEAD_DIM]
                if g_ref is not None:
                    a = _rms(a, g_ref[...])
                o_ref[which * A_HPG + hh, 0, rows, :] = a.astype(o_ref.dtype)


def _attn_proj_norm(x, ng, w, gq, gk, l, batch, seq, tm=1024):
    m, d_model = x.shape
    planes = 3 * A_HPG
    tiles_per_seq = seq // tm
    w_spec = lambda which: pl.BlockSpec((None, d_model, A_WIDTH), lambda b, t: (l, 0, which * N_A_GROUPS))
    row = lambda b, t: b * tiles_per_seq + t
    return pl.pallas_call(
        _attn_proj_norm_kernel,
        grid=(batch, tiles_per_seq),
        in_specs=[pl.BlockSpec((tm, d_model), lambda b, t: (row(b, t), 0)),
                  pl.BlockSpec((None, 1, d_model), lambda b, t: (l, 0, 0)),
                  w_spec(0), w_spec(1), w_spec(2),
                  pl.BlockSpec((None, 1, A_HEAD_DIM), lambda b, t: (l, 0, 0)),
                  pl.BlockSpec((None, 1, A_HEAD_DIM), lambda b, t: (l, 0, 0))],
        out_specs=[pl.BlockSpec((planes, None, 1, tm, A_HEAD_DIM), lambda b, t: (0, b, 0, t, 0)),
                   pl.BlockSpec((tm, d_model), lambda b, t: (row(b, t), 0))],
        out_shape=[jax.ShapeDtypeStruct((planes, batch, 1, seq, A_HEAD_DIM), BF16),
                   jax.ShapeDtypeStruct((m, d_model), BF16)],
        compiler_params=_cparams(2),
        name="attn_proj_norm_g0",
    )(x, ng, w, w, w, gq, gk)


def _gla_proj_kernel(h_ref, w_ref, o_ref, gl_ref):
    acc = jnp.dot(h_ref[...], w_ref[...], preferred_element_type=F32)
    rows = acc.shape[0]

    def plane(off, width):
        a = acc[:, off:off + width]
        if width < 128:
            a = jnp.concatenate([a, jnp.zeros((rows, 128 - width), F32)], axis=1)
        return a

    for hh in range(GLA_HEADS):
        o_ref[hh] = plane(hh * GLA_DK, GLA_DK).astype(o_ref.dtype)
        o_ref[GLA_HEADS + hh] = plane(GLA_HEADS * GLA_DK + hh * GLA_DK, GLA_DK).astype(o_ref.dtype)
        v_off = 2 * GLA_HEADS * GLA_DK + hh * GLA_DV
        o_ref[2 * GLA_HEADS + 2 * hh] = plane(v_off, 128).astype(o_ref.dtype)
        o_ref[2 * GLA_HEADS + 2 * hh + 1] = plane(v_off + 128, GLA_DV - 128).astype(o_ref.dtype)
    gl_ref[...] = plane(2 * GLA_HEADS * GLA_DK + GLA_HEADS * GLA_DV, GLA_RANK)


def _gla_proj(h, w, l, tm=512):
    m, d = h.shape
    n = W_GLA_COLS
    n_blk = 4 * GLA_HEADS
    return pl.pallas_call(
        _gla_proj_kernel,
        grid=(m // tm,),
        in_specs=[pl.BlockSpec((tm, d), lambda i: (i, 0)),
                  pl.BlockSpec((None, d, n), lambda i: (l, 0, 0))],
        out_specs=[pl.BlockSpec((n_blk, tm, 128), lambda i: (0, i, 0)),
                   pl.BlockSpec((tm, 128), lambda i: (i, 0))],
        out_shape=[jax.ShapeDtypeStruct((n_blk, m, 128), BF16),
                   jax.ShapeDtypeStruct((m, 128), F32)],
        compiler_params=_cparams(1),
        name="gla_proj",
    )(h, w)


def _sgu_tile(h_ref, w_ref, lng_ref, lnb_ref, ws_ref, bs_ref, o_ref):
    acc = jnp.dot(h_ref[...], w_ref[...], preferred_element_type=F32)
    u = jax.nn.gelu(acc[:, :SGU_WIDTH])
    gv = jax.nn.gelu(acc[:, SGU_WIDTH:])
    mu = jnp.mean(gv, axis=-1, keepdims=True)
    xc = gv - mu
    var = jnp.mean(xc * xc, axis=-1, keepdims=True)
    vh = (xc * lax.rsqrt(var + EPS) * lng_ref[...] + lnb_ref[...]).astype(BF16)
    half = 2 * SGU_GW
    lane = lax.broadcasted_iota(jnp.int32, (SGU_CHUNK, half), 1)
    first = lane < SGU_GW
    for c in range(h_ref.shape[0] // SGU_CHUNK):
        rows = slice(c * SGU_CHUNK, (c + 1) * SGU_CHUNK)
        for p in range(2):
            cols = slice(p * half, (p + 1) * half)
            vc = vh[rows, cols]
            r0 = jnp.dot(ws_ref[2 * p], vc, preferred_element_type=F32)
            r1 = jnp.dot(ws_ref[2 * p + 1], vc, preferred_element_type=F32)
            f = jnp.where(first, r0, r1) + bs_ref[:, cols]
            o_ref[rows, cols] = (u[rows, cols] * f).astype(o_ref.dtype)


def _t5_causal_bucket(dist):
    max_exact = REL_BUCKETS // 2
    d = np.maximum(dist, 1)
    large = max_exact + (np.log(d / max_exact) / np.log(REL_MAX_DIST / max_exact)
                         * (REL_BUCKETS - max_exact)).astype(np.int64)
    large = np.minimum(large, REL_BUCKETS - 1)
    return np.where(dist < max_exact, dist, large).astype(np.int32)


def _attn_bias_tiles(rel_bias):
    dist = np.arange(A_BLK + 1)
    period = 2 * A_BLK + 1
    tiles = []
    for gi, dilation in enumerate(DILATIONS):
        onehot = np.eye(REL_BUCKETS, dtype=np.float32)[_t5_causal_bucket(dist * dilation)]
        heads = rel_bias[:, gi * A_HPG:(gi + 1) * A_HPG].astype(F32)
        vec = jnp.dot(jnp.asarray(onehot), heads, precision=lax.Precision.HIGHEST).T
        seq = jnp.concatenate([vec[:, ::-1], jnp.full((A_HPG, period - A_BLK - 1), NEG, F32)], axis=1)
        flat = jnp.tile(seq, (1, A_BLK))[:, :A_BLK * 2 * A_BLK]
        tiles.append(flat.reshape(A_HPG, A_BLK, 2 * A_BLK))
    return jnp.stack(tiles, axis=0)


def _attn_unit(q, k, v, bias, scale, masked=None):
    s = lax.dot_general(q, k, (((1,), (1,)), ((), ())), preferred_element_type=F32) * scale + bias
    if masked is not None:
        s = jnp.where(masked, NEG, s)
    mx = jnp.max(s, axis=-1, keepdims=True)
    p = jnp.exp(s - mx)
    den = jnp.sum(p, axis=-1, keepdims=True)
    o = jnp.dot(p.astype(BF16), v, preferred_element_type=F32) / den
    return o, mx + jnp.log(den)


def _attn_sgu_kernel(*refs):
    groups = [refs[5 * gi:5 * gi + 5] for gi in range(N_A_GROUPS)]
    bias_ref = refs[15]
    sgu_in = refs[16:22]
    o_ref, oc_ref, og, lg = refs[22:]

    scale = A_HEAD_DIM ** -0.5
    col = lax.broadcasted_iota(jnp.int32, (A_BLK, 2 * A_BLK), 1)
    no_prev = jnp.logical_and(pl.program_id(2) == 0, col < A_BLK)
    for gi, (q_ref, k_ref, v_ref, kp_ref, vp_ref) in enumerate(groups):
        dil = DILATIONS[gi]
        n_blk = q_ref.shape[1] // A_BLK
        bias = bias_ref[gi]
        for r in range(dil):
            for n in range(n_blk):
                q = q_ref[r, n * A_BLK:(n + 1) * A_BLK, :]
                if n == 0:
                    k = jnp.concatenate([kp_ref[r], k_ref[r, 0:A_BLK, :]], axis=0)
                    v = jnp.concatenate([vp_ref[r], v_ref[r, 0:A_BLK, :]], axis=0)
                    o, lse = _attn_unit(q, k, v, bias, scale, masked=no_prev)
                else:
                    k = k_ref[r, (n - 1) * A_BLK:(n + 1) * A_BLK, :]
                    v = v_ref[r, (n - 1) * A_BLK:(n + 1) * A_BLK, :]
                    o, lse = _attn_unit(q, k, v, bias, scale)
                start = r + dil * A_BLK * n
                rows = pl.ds(start, A_BLK) if dil == 1 else pl.ds(start, A_BLK, stride=dil)
                og[gi, rows, :] = o
                lg[gi, rows, :] = jnp.broadcast_to(lse, (A_BLK, A_HEAD_DIM))
    _sgu_tile(*sgu_in, oc_ref)
    l0, l1, l2 = lg[0], lg[1], lg[2]
    mx = jnp.maximum(jnp.maximum(l0, l1), l2)
    w0, w1, w2 = jnp.exp(l0 - mx), jnp.exp(l1 - mx), jnp.exp(l2 - mx)
    o = (w0 * og[0] + w1 * og[1] + w2 * og[2]) / (w0 + w1 + w2)
    o_ref[...] = o.astype(o_ref.dtype)


def _attn_group_specs(dil, part):
    rows = part // dil
    blocks_per_part = rows // A_BLK

    def cur(which):
        return pl.BlockSpec((None, None, dil, rows, A_HEAD_DIM),
                            lambda b, hh, s: (which * A_HPG + hh, b, 0, s, 0))

    def prev(which):
        return pl.BlockSpec((None, None, dil, A_BLK, A_HEAD_DIM),
                            lambda b, hh, s: (which * A_HPG + hh, b, 0,
                                              jnp.maximum(s * blocks_per_part - 1, 0), 0))

    return [cur(0), cur(1), cur(2), prev(1), prev(2)]


def _attn_sgu(qkvs, bias, h, w, ln_g, ln_b, ws, bs_tile, l, batch, seq):
    m, d_model = h.shape
    part = seq // ATTN_SPLIT
    sgu_rows = m // (batch * A_HPG * ATTN_SPLIT)
    in_specs, args = [], []
    for gi, qkv in enumerate(qkvs):
        in_specs += _attn_group_specs(DILATIONS[gi], part)
        args += [qkv] * 5
    tile = lambda b, hh, s: (b * A_HPG + hh) * ATTN_SPLIT + s
    in_specs += [
        pl.BlockSpec((N_A_GROUPS, None, A_BLK, 2 * A_BLK), lambda b, hh, s: (0, hh, 0, 0)),
        pl.BlockSpec((sgu_rows, d_model), lambda b, hh, s: (tile(b, hh, s), 0)),
        pl.BlockSpec((None, d_model, 2 * SGU_WIDTH), lambda b, hh, s: (l, 0, 0),
                     pipeline_mode=pl.Buffered(1)),
        pl.BlockSpec((None, 1, SGU_WIDTH), lambda b, hh, s: (l, 0, 0)),
        pl.BlockSpec((None, 1, SGU_WIDTH), lambda b, hh, s: (l, 0, 0)),
        pl.BlockSpec((None, SGU_GROUPS, SGU_CHUNK, SGU_CHUNK), lambda b, hh, s: (l, 0, 0, 0)),
        pl.BlockSpec((None, SGU_CHUNK, SGU_WIDTH), lambda b, hh, s: (l, 0, 0))]
    args += [bias, h, w, ln_g, ln_b, ws, bs_tile]
    return pl.pallas_call(
        _attn_sgu_kernel,
        grid=(batch, A_HPG, ATTN_SPLIT),
        in_specs=in_specs,
        out_specs=[pl.BlockSpec((None, part, A_HEAD_DIM), lambda b, hh, s: (hh, b * ATTN_SPLIT + s, 0)),
                   pl.BlockSpec((sgu_rows, SGU_WIDTH), lambda b, hh, s: (tile(b, hh, s), 0))],
        out_shape=[jax.ShapeDtypeStruct((A_HPG, m, A_HEAD_DIM), BF16),
                   jax.ShapeDtypeStruct((m, SGU_WIDTH), BF16)],
        scratch_shapes=[pltpu.VMEM((N_A_GROUPS, part, A_HEAD_DIM), F32),
                        pltpu.VMEM((N_A_GROUPS, part, A_HEAD_DIM), F32)],
        compiler_params=_cparams(3),
        name="attn_sgu",
    )(*args)


def _gla_static_matrices(tile):
    t = np.arange(tile)
    same = (t[:, None] // GLA_CHUNK) == (t[None, :] // GLA_CHUNK)
    inc = same & (t[None, :] <= t[:, None])
    a = np.arange(GLA_ATILE)
    ti, si = a[:, None], a[None, :]
    masks = []
    for m in GLA_LEVELS:
        masks.append((ti // (2 * m) == si // (2 * m)) & ((ti // m) % 2 == 1) & ((si // m) % 2 == 0))
    return jnp.asarray(inc, BF16), jnp.asarray(np.stack(masks), F32)


def _gla_kernel(q_ref, k_ref, v_ref, gl_ref, wh_ref, wl_ref, bg_ref, gout_ref, inc_ref, lvl_ref,
                h_ref, wg_ref, o_ref, gates_ref, st_ref):
    @pl.when(pl.program_id(1) == 0)
    def _():
        st_ref[...] = jnp.zeros_like(st_ref)

    n_gate_chunks = gates_ref.shape[1] // GATE_CHUNK

    def gate_chunks(first, last):
        for c in range(first, last):
            cols = slice(c * GATE_CHUNK, (c + 1) * GATE_CHUNK)
            acc = jnp.dot(h_ref[...], wg_ref[:, cols], preferred_element_type=F32)
            gates_ref[:, cols] = jax.nn.sigmoid(acc).astype(gates_ref.dtype)

    t_rows = gl_ref.shape[0]
    n_sub = t_rows // 8
    width = GLA_HEADS * GLA_DK_PAD

    def split(x):
        hi = x.astype(BF16)
        return hi, (x - hi.astype(F32)).astype(BF16)

    g_hi, g_lo = split(gl_ref[...])
    wh, wl = wh_ref[...], wl_ref[...]
    x = (jnp.dot(g_hi, wh, preferred_element_type=F32) + jnp.dot(g_hi, wl, preferred_element_type=F32)
         + jnp.dot(g_lo, wh, preferred_element_type=F32) + bg_ref[...])
    gate_chunks(0, n_gate_chunks // 3)
    la =(jnp.minimum(x, 0.0) - jnp.log1p(jnp.exp(-jnp.abs(x)))) * (1.0 / GLA_TAU)
    la_hi, la_lo = split(la)
    inc = inc_ref[...]
    b = jnp.dot(inc, la_hi, preferred_element_type=F32) + jnp.dot(inc, la_lo, preferred_element_type=F32)
    gate_chunks(n_gate_chunks // 3, 2 * n_gate_chunks // 3)

    b3 = b.reshape(n_sub, 8, width)
    la3 = la.reshape(n_sub, 8, width)
    sub = lax.broadcasted_iota(jnp.int32, (1, 8, width), 1)

    def hi_step(y, bit):
        return jnp.where((sub & bit) == 0, pltpu.roll(y, 8 - bit, axis=1), y)

    def lo_step(y, bit):
        return jnp.where((sub & bit) != 0, pltpu.roll(y, bit, axis=1), y)

    def spread(y, groups, pick):
        y4 = y.reshape(n_sub // groups, groups, 8, width)
        return jnp.broadcast_to(y4[:, pick:pick + 1], y4.shape).reshape(n_sub, 8, width)

    hi = {1: b3}
    lo = {1: b3 - la3}
    for bit in (1, 2, 4):
        hi[2 * bit] = hi_step(hi[bit], bit)
        lo[2 * bit] = lo_step(lo[bit], bit)
    for groups in (2, 4, 8):
        hi[8 * groups] = spread(hi[8], groups, groups - 1)
        lo[8 * groups] = spread(lo[8], groups, 0)

    q_all = jnp.concatenate([q_ref[h] for h in range(GLA_HEADS)], axis=1).astype(F32) * (GLA_DK ** -0.5)
    k_all = jnp.concatenate([k_ref[h] for h in range(GLA_HEADS)], axis=1).astype(F32)
    q3 = q_all.reshape(n_sub, 8, width)
    k3 = k_all.reshape(n_sub, 8, width)

    def q_side(m):
        return (q3 * jnp.exp(b3 - lo[m])).reshape(t_rows, width).astype(BF16)

    def k_side(m):
        return (k3 * jnp.exp(hi[m] - b3)).reshape(t_rows, width).astype(BF16)

    qd = {m: q_side(m) for m in GLA_LEVELS + (GLA_CHUNK,)}
    kd = {m: (k_all.astype(BF16) if m == 1 else k_side(m)) for m in GLA_LEVELS + (GLA_CHUNK,)}
    diag = q_all * k_all
    v_heads = [jnp.concatenate([v_ref[2 * h], v_ref[2 * h + 1]], axis=1) for h in range(GLA_HEADS)]
    g_out = gout_ref[...]
    nt = (((1,), (1,)), ((), ()))
    chunks_per_atile = GLA_ATILE // GLA_CHUNK

    for a in range(t_rows // GLA_ATILE):
        if a == 1:
            gate_chunks(2 * n_gate_chunks // 3, n_gate_chunks)
        arows = slice(a * GLA_ATILE, (a + 1) * GLA_ATILE)
        for h in range(GLA_HEADS):
            cols = slice(h * GLA_DK_PAD, (h + 1) * GLA_DK_PAD)
            scores = None
            for li, m in enumerate(GLA_LEVELS):
                s = lax.dot_general(qd[m][arows, cols], kd[m][arows, cols], nt, preferred_element_type=F32)
                s = s * lvl_ref[li]
                scores = s if scores is None else scores + s
            v_a = v_heads[h][arows]
            rd = jnp.sum(diag[arows, cols], axis=-1, keepdims=True)
            intra = jnp.dot(scores.astype(BF16), v_a, preferred_element_type=F32) + rd * v_a.astype(F32)
            for cc in range(chunks_per_atile):
                c = a * chunks_per_atile + cc
                rows = slice(c * GLA_CHUNK, (c + 1) * GLA_CHUNK)
                st = st_ref[h]
                inter = lax.dot_general(qd[GLA_CHUNK][rows, cols], st.astype(BF16), nt,
                                        preferred_element_type=F32)
                o = intra[cc * GLA_CHUNK:(cc + 1) * GLA_CHUNK] + inter
                ms = jnp.sum(o * o, axis=-1, keepdims=True) * (1.0 / GLA_DV)
                o_n = (o * lax.rsqrt(ms + EPS) * g_out).astype(o_ref.dtype)
                o_ref[rows, h * GLA_DV:(h + 1) * GLA_DV] = o_n[:, :GLA_DV]
                last = c * GLA_CHUNK + GLA_CHUNK - 1
                decay = jnp.exp(b[last:last + 1, cols])
                upd = lax.dot_general(v_heads[h][rows], kd[GLA_CHUNK][rows, cols],
                                      (((0,), (0,)), ((), ())), preferred_element_type=F32)
                st_ref[h] = st * decay + upd


def _gla(glaqkv, gl, wup_hi, wup_lo, bg, gout, h, w_gates, l, batch, seq):
    m = gl.shape[0]
    d_model = h.shape[1]
    n_gates = N_GATE_COLS
    tile = GLA_TILE
    tiles_per_seq = seq // tile
    inc, lvl = _gla_static_matrices(tile)
    row = lambda b, t: b * tiles_per_seq + t
    width = GLA_HEADS * GLA_DK_PAD
    return pl.pallas_call(
        _gla_kernel,
        grid=(batch, tiles_per_seq),
        in_specs=[pl.BlockSpec((GLA_HEADS, tile, 128), lambda b, t: (0, row(b, t), 0)),
                  pl.BlockSpec((GLA_HEADS, tile, 128), lambda b, t: (1, row(b, t), 0)),
                  pl.BlockSpec((2 * GLA_HEADS, tile, 128), lambda b, t: (1, row(b, t), 0)),
                  pl.BlockSpec((tile, 128), lambda b, t: (row(b, t), 0)),
                  pl.BlockSpec((None, 128, width), lambda b, t: (l, 0, 0)),
                  pl.BlockSpec((None, 128, width), lambda b, t: (l, 0, 0)),
                  pl.BlockSpec((None, 1, width), lambda b, t: (l, 0, 0)),
                  pl.BlockSpec((None, 1, GLA_DV_PAD), lambda b, t: (l, 0, 0)),
                  pl.BlockSpec((tile, tile), lambda b, t: (0, 0)),
                  pl.BlockSpec((len(GLA_LEVELS), GLA_ATILE, GLA_ATILE), lambda b, t: (0, 0, 0)),
                  pl.BlockSpec((tile, d_model), lambda b, t: (row(b, t), 0)),
                  pl.BlockSpec((None, d_model, n_gates), lambda b, t: (l, 0, 0),
                               pipeline_mode=pl.Buffered(1))],
        out_specs=[pl.BlockSpec((tile, GLA_HEADS * GLA_DV), lambda b, t: (row(b, t), 0)),
                   pl.BlockSpec((tile, n_gates), lambda b, t: (row(b, t), 0))],
        out_shape=[jax.ShapeDtypeStruct((m, GLA_HEADS * GLA_DV), BF16),
                   jax.ShapeDtypeStruct((m, n_gates), BF16)],
        scratch_shapes=[pltpu.VMEM((GLA_HEADS, GLA_DV_PAD, GLA_DK_PAD), F32)],
        compiler_params=_cparams(2),
        name="gla_gates",
    )(glaqkv, glaqkv, glaqkv, gl, wup_hi, wup_lo, bg, gout, inc, lvl, h, w_gates)


def _merge_kernel(oa_ref, ob_ref, oc_ref, ga_ref, gb_ref, gc_ref, pa_ref, pb_ref, pc_ref, y_ref):
    oa = jnp.concatenate([oa_ref[hh] for hh in range(A_HPG)], axis=1)
    ya = jnp.dot(oa, pa_ref[...], preferred_element_type=F32)
    yb = jnp.dot(ob_ref[...], pb_ref[...], preferred_element_type=F32)
    yc = jnp.dot(oc_ref[...], pc_ref[...], preferred_element_type=F32)
    y = (ga_ref[...].astype(F32) * ya + gb_ref[...].astype(F32) * yb + gc_ref[...].astype(F32) * yc)
    y_ref[...] = y.astype(y_ref.dtype)


def _merge(oa, ob, oc, gates, pa, pb, pc, l, tm=256):
    m = ob.shape[0]
    n = pa.shape[2]
    const = lambda i: (l, 0, 0)
    return pl.pallas_call(
        _merge_kernel,
        grid=(m // tm,),
        in_specs=[pl.BlockSpec((A_HPG, tm, A_HEAD_DIM), lambda i: (0, i, 0)),
                  pl.BlockSpec((tm, ob.shape[1]), lambda i: (i, 0)),
                  pl.BlockSpec((tm, oc.shape[1]), lambda i: (i, 0)),
                  pl.BlockSpec((tm, n), lambda i: (i, 0)),
                  pl.BlockSpec((tm, n), lambda i: (i, 1)),
                  pl.BlockSpec((tm, n), lambda i: (i, 2)),
                  pl.BlockSpec((None,) + pa.shape[1:], const),
                  pl.BlockSpec((None,) + pb.shape[1:], const),
                  pl.BlockSpec((None,) + pc.shape[1:], const)],
        out_specs=pl.BlockSpec((tm, n), lambda i: (i, 0)),
        out_shape=jax.ShapeDtypeStruct((m, n), BF16),
        compiler_params=_cparams(1),
        name="merge",
    )(oa, ob, oc, gates, gates, gates, pa, pb, pc)


def _mm_res_kernel(a_ref, w_ref, x_ref, o_ref):
    k = pl.program_id(2)
    acc = jnp.dot(a_ref[...], w_ref[...], preferred_element_type=F32)

    @pl.when(k == 0)
    def _():
        o_ref[...] = x_ref[...] + acc

    @pl.when(k > 0)
    def _():
        o_ref[...] += acc


def _mm_res(a, w, x, l, tm=1024, tn=1024, tk=None):
    m, kdim = a.shape
    n = w.shape[2]
    tk = kdim if tk is None else tk
    return pl.pallas_call(
        _mm_res_kernel,
        grid=(m // tm, n // tn, kdim // tk),
        in_specs=[pl.BlockSpec((tm, tk), lambda i, j, k: (i, k)),
                  pl.BlockSpec((None, tk, tn), lambda i, j, k: (l, k, j)),
                  pl.BlockSpec((tm, tn), lambda i, j, k: (i, j))],
        out_specs=pl.BlockSpec((tm, tn), lambda i, j, k: (i, j)),
        out_shape=jax.ShapeDtypeStruct((m, n), F32),
        compiler_params=_cparams(3),
        name="mm_res",
    )(a, w, x)


def _ffn_in_kernel(x_ref, g_ref, wg_ref, wu_ref, o_ref, h_scr):
    def tile(h):
        g = jnp.dot(h, wg_ref[...], preferred_element_type=F32)
        u = jnp.dot(h, wu_ref[...], preferred_element_type=F32)
        return (g * jax.nn.sigmoid(g) * u).astype(o_ref.dtype)

    @pl.when(pl.program_id(1) == 0)
    def _():
        for c in range(x_ref.shape[0] // NORM_CHUNK):
            rows = slice(c * NORM_CHUNK, (c + 1) * NORM_CHUNK)
            h = _rms(x_ref[rows, :], g_ref[...]).astype(h_scr.dtype)
            h_scr[rows, :] = h
            o_ref[rows, :] = tile(h)

    @pl.when(pl.program_id(1) > 0)
    def _():
        o_ref[...] = tile(h_scr[...])


def _ffn_in(x, g, w, l, tm=1024, tn=512):
    m, d = x.shape
    f = w.shape[2] // 2
    up_off = f // tn
    return pl.pallas_call(
        _ffn_in_kernel,
        grid=(m // tm, f // tn),
        in_specs=[pl.BlockSpec((tm, d), lambda i, j: (i, 0)),
                  pl.BlockSpec((None, 1, d), lambda i, j: (l, 0, 0)),
                  pl.BlockSpec((None, d, tn), lambda i, j: (l, 0, j)),
                  pl.BlockSpec((None, d, tn), lambda i, j: (l, 0, up_off + j))],
        out_specs=pl.BlockSpec((tm, tn), lambda i, j: (i, j)),
        out_shape=jax.ShapeDtypeStruct((m, f), BF16),
        scratch_shapes=[pltpu.VMEM((tm, d), BF16)],
        compiler_params=_cparams(2),
        name="ffn_in",
    )(x, g, w, w)


def _pad_heads(w, off, n_heads, width, padded):
    lead = w.shape[:-1]
    cols = w[..., off:off + n_heads * width].reshape(lead + (n_heads, width))
    cols = jnp.pad(cols, [(0, 0)] * (len(lead) + 1) + [(0, padded - width)])
    return cols.reshape(lead + (n_heads * padded,))


def _prepare_params(w_in, gla_gate_up, gla_gate_b, gla_out_g, sgu_w, sgu_b, w_branch):
    depth = w_in.shape[0]
    w_in_b = w_in.astype(BF16)
    w_attn = w_in_b[:, :, _OFF_AQ:_OFF_BQ]
    n_gla = _OFF_CU - _OFF_BQ
    w_gla = jnp.pad(w_in_b[:, :, _OFF_BQ:_OFF_CU], ((0, 0), (0, 0), (0, W_GLA_COLS - n_gla)))
    w_sgu = w_in_b[:, :, _OFF_CU:_OFF_GATES]
    w_gates = w_in_b[:, :, _OFF_GATES:]
    wup = _pad_heads(gla_gate_up, 0, GLA_HEADS, GLA_DK, GLA_DK_PAD)
    wup = jnp.pad(wup, ((0, 0), (0, 128 - GLA_RANK), (0, 0)))
    wup_hi = wup.astype(BF16)
    wup_lo = (wup - wup_hi.astype(F32)).astype(BF16)
    bg = _pad_heads(gla_gate_b.reshape(depth, 1, -1), 0, GLA_HEADS, GLA_DK, GLA_DK_PAD)
    gout = jnp.pad(gla_out_g.reshape(depth, 1, GLA_DV), ((0, 0), (0, 0), (0, GLA_DV_PAD - GLA_DV)))
    tril = np.tril(np.ones((SGU_CHUNK, SGU_CHUNK), dtype=bool))
    ws = jnp.where(tril, sgu_w, 0.0).astype(BF16)
    bs_tile = jnp.repeat(jnp.swapaxes(sgu_b, 1, 2), SGU_GW, axis=2)
    pa = w_branch[:, :A_WIDTH].astype(BF16)
    pb = w_branch[:, A_WIDTH:A_WIDTH + GLA_HEADS * GLA_DV].astype(BF16)
    pc = w_branch[:, A_WIDTH + GLA_HEADS * GLA_DV:].astype(BF16)
    return dict(w_attn=w_attn, w_gla=w_gla, w_sgu=w_sgu, w_gates=w_gates, wup_hi=wup_hi, wup_lo=wup_lo,
                bg=bg, gout=gout, ws=ws, bs_tile=bs_tile, pa=pa, pb=pb, pc=pc)


def kernel(x, rel_bias, norm1_g, w_in, q_norm_g, k_norm_g, gla_gate_up, gla_gate_b, gla_out_g,
           sgu_ln_g, sgu_ln_b, sgu_w, sgu_b, w_branch, w_out, norm2_g, w_ffn_in, w_ffn_out):
    batch, seq, d = x.shape
    depth = w_in.shape[0]
    m = batch * seq
    assert seq % (A_BLK * max(DILATIONS)) == 0 and seq % GLA_TILE == 0 and d == D_MODEL
    p = _prepare_params(w_in, gla_gate_up, gla_gate_b, gla_out_g, sgu_w, sgu_b, w_branch)
    w_out_b = w_out.astype(BF16)
    w_ffn_in_b = w_ffn_in.astype(BF16)
    w_ffn_out_b = w_ffn_out.astype(BF16)
    n1 = norm1_g.reshape(depth, 1, d)
    n2 = norm2_g.reshape(depth, 1, d)
    gq = q_norm_g.reshape(depth, 1, A_HEAD_DIM)
    gk = k_norm_g.reshape(depth, 1, A_HEAD_DIM)
    ln_g = sgu_ln_g.reshape(depth, 1, SGU_WIDTH)
    ln_b = sgu_ln_b.reshape(depth, 1, SGU_WIDTH)
    bias = _attn_bias_tiles(rel_bias)
    xf = x.reshape(m, d)
    for l in range(depth):
        qkv0, h = _attn_proj_norm(xf, n1, p["w_attn"], gq, gk, l, batch, seq)
        qkvs = [qkv0] + [_attn_proj(h, p["w_attn"], gq, gk, l, gi, batch, seq) for gi in (1, 2)]
        glaqkv, gl = _gla_proj(h, p["w_gla"], l)
        o_a, o_c = _attn_sgu(qkvs, bias, h, p["w_sgu"], ln_g, ln_b, p["ws"], p["bs_tile"], l, batch, seq)
        o_b, gates = _gla(glaqkv, gl, p["wup_hi"], p["wup_lo"], p["bg"], p["gout"], h, p["w_gates"],
                          l, batch, seq)
        y = _merge(o_a, o_b, o_c, gates, p["pa"], p["pb"], p["pc"], l)
        xf = _mm_res(y, w_out_b, xf, l, tm=512, tn=D_MODEL)
        a = _ffn_in(xf, n2, w_ffn_in_b, l)
        xf = _mm_res(a, w_ffn_out_b, xf, l, tm=1024, tn=512)
    return xf.reshape(batch, seq, d)
```

```python
import functools

import numpy as np
import jax
import jax.numpy as jnp
from jax import lax
from jax.experimental import pallas as pl
from jax.experimental.pallas import tpu as pltpu

F32 = jnp.float32
BF16 = jnp.bfloat16

D_MODEL = 2048
DILATIONS = (1, 4, 16)
N_A_GROUPS = 3
A_HEAD_DIM = 128
A_WIDTH = 512
A_HPG = 4
A_BLK = 128
GLA_HEADS = 4
GLA_DK = 96
GLA_DV = 192
GLA_DK_PAD = 128
GLA_DV_PAD = 256
GLA_RANK = 16
GLA_TAU = 16.0
GLA_CHUNK = 64
GLA_LEVELS = (1, 2, 4, 8, 16, 32)
GLA_ATILE = 128
SGU_WIDTH = 768
SGU_GROUPS = 4
SGU_GW = SGU_WIDTH // SGU_GROUPS
SGU_CHUNK = 128
D_FFN = 5632
REL_BUCKETS = 32
REL_MAX_DIST = 2048
EPS = 1e-6
NEG = -1e30

_OFF_AQ, _OFF_AK, _OFF_AV = 0, 1536, 3072
_OFF_BQ, _OFF_BK, _OFF_BV = 4608, 4992, 5376
_OFF_GL = 6144
_OFF_CU, _OFF_CV = 6160, 6928
_OFF_GATES = 7696

VMEM_LIMIT_BYTES = 56 * 1024 * 1024
VMEM_LIMIT_BIG_BYTES = 60 * 1024 * 1024

GLA_TILE = 256
NORM_CHUNK = 256
GATE_CHUNK = 1024
ATTN_SPLIT = 2
N_GATE_COLS = 3 * D_MODEL
W_GLA_COLS = 1664


def _cparams(n_axes, vmem_limit_bytes=VMEM_LIMIT_BYTES):
    return pltpu.CompilerParams(
        dimension_semantics=("arbitrary",) * n_axes,
        vmem_limit_bytes=vmem_limit_bytes,
    )


def _rms(x, g):
    ms = jnp.mean(x * x, axis=-1, keepdims=True)
    return x * lax.rsqrt(ms + EPS) * g


def _attn_proj_kernel(h_ref, wq_ref, wk_ref, wv_ref, gq_ref, gk_ref, o_ref, *scratch, dil):
    h = h_ref[...]
    sub_rows = h.shape[0] // dil
    for which, (w_ref, g_ref) in enumerate(((wq_ref, gq_ref), (wk_ref, gk_ref), (wv_ref, None))):
        acc = jnp.dot(h, w_ref[...], preferred_element_type=F32)
        for hh in range(A_HPG):
            p = which * A_HPG + hh
            a = acc[:, hh * A_HEAD_DIM:(hh + 1) * A_HEAD_DIM]
            if g_ref is not None:
                a = _rms(a, g_ref[...])
            if dil == 4:
                scr = scratch[0]
                scr[p] = a
                for r in range(dil):
                    o_ref[p, r] = scr[p, pl.ds(r, sub_rows, stride=dil), :].astype(o_ref.dtype)
            else:
                scr, scr2 = scratch
                scr[p] = a
                quarter = h.shape[0] // 4
                for r1 in range(4):
                    scr2[p, r1] = scr[p, pl.ds(r1, quarter, stride=4), :]
                    for r2 in range(4):
                        o_ref[p, r1 + 4 * r2] = scr2[p, r1, pl.ds(r2, sub_rows, stride=4), :].astype(o_ref.dtype)


def _attn_proj(h, w, gq, gk, l, gi, batch, seq, tm=1024):
    d_model = h.shape[1]
    dil = DILATIONS[gi]
    planes = 3 * A_HPG
    tiles_per_seq = seq // tm
    w_spec = lambda which: pl.BlockSpec((None, d_model, A_WIDTH),
                                        lambda b, t: (l, 0, which * N_A_GROUPS + gi))
    return pl.pallas_call(
        functools.partial(_attn_proj_kernel, dil=dil),
        grid=(batch, tiles_per_seq),
        in_specs=[pl.BlockSpec((tm, d_model), lambda b, t: (b * tiles_per_seq + t, 0)),
                  w_spec(0), w_spec(1), w_spec(2),
                  pl.BlockSpec((None, 1, A_HEAD_DIM), lambda b, t: (l, 0, 0)),
                  pl.BlockSpec((None, 1, A_HEAD_DIM), lambda b, t: (l, 0, 0))],
        out_specs=pl.BlockSpec((planes, None, dil, tm // dil, A_HEAD_DIM), lambda b, t: (0, b, 0, t, 0)),
        out_shape=jax.ShapeDtypeStruct((planes, batch, dil, seq // dil, A_HEAD_DIM), BF16),
        scratch_shapes=([pltpu.VMEM((planes, tm, A_HEAD_DIM), F32)]
                        + ([pltpu.VMEM((planes, 4, tm // 4, A_HEAD_DIM), F32)] if dil == 16 else [])),
        compiler_params=_cparams(2),
        name=f"attn_proj_g{gi}",
    )(h, w, w, w, gq, gk)


def _attn_proj_norm_kernel(x_ref, ng_ref, wq_ref, wk_ref, wv_ref, gq_ref, gk_ref, o_ref, h_ref):
    for c in range(x_ref.shape[0] // NORM_CHUNK):
        rows = slice(c * NORM_CHUNK, (c + 1) * NORM_CHUNK)
        h = _rms(x_ref[rows, :], ng_ref[...]).astype(h_ref.dtype)
        h_ref[rows, :] = h
        for which, (w_ref, g_ref) in enumerate(((wq_ref, gq_ref), (wk_ref, gk_ref), (wv_ref, None))):
            acc = jnp.dot(h, w_ref[...], preferred_element_type=F32)
            for hh in range(A_HPG):
                a = acc[:, hh * A_HEAD_DIM:(hh + 1) * A_HEAD_DIM]
                if g_ref is not None:
                    a = _rms(a, g_ref[...])
                o_ref[which * A_HPG + hh, 0, rows, :] = a.astype(o_ref.dtype)


def _attn_proj_norm(x, ng, w, gq, gk, l, batch, seq, tm=1024):
    m, d_model = x.shape
    planes = 3 * A_HPG
    tiles_per_seq = seq // tm
    w_spec = lambda which: pl.BlockSpec((None, d_model, A_WIDTH), lambda b, t: (l, 0, which * N_A_GROUPS))
    row = lambda b, t: b * tiles_per_seq + t
    return pl.pallas_call(
        _attn_proj_norm_kernel,
        grid=(batch, tiles_per_seq),
        in_specs=[pl.BlockSpec((tm, d_model), lambda b, t: (row(b, t), 0)),
                  pl.BlockSpec((None, 1, d_model), lambda b, t: (l, 0, 0)),
                  w_spec(0), w_spec(1), w_spec(2),
                  pl.BlockSpec((None, 1, A_HEAD_DIM), lambda b, t: (l, 0, 0)),
                  pl.BlockSpec((None, 1, A_HEAD_DIM), lambda b, t: (l, 0, 0))],
        out_specs=[pl.BlockSpec((planes, None, 1, tm, A_HEAD_DIM), lambda b, t: (0, b, 0, t, 0)),
                   pl.BlockSpec((tm, d_model), lambda b, t: (row(b, t), 0))],
        out_shape=[jax.ShapeDtypeStruct((planes, batch, 1, seq, A_HEAD_DIM), BF16),
                   jax.ShapeDtypeStruct((m, d_model), BF16)],
        compiler_params=_cparams(2),
        name="attn_proj_norm_g0",
    )(x, ng, w, w, w, gq, gk)


def _gla_proj_kernel(h_ref, w_ref, o_ref, gl_ref):
    acc = jnp.dot(h_ref[...], w_ref[...], preferred_element_type=F32)
    rows = acc.shape[0]

    def plane(off, width):
        a = acc[:, off:off + width]
        if width < 128:
            a = jnp.concatenate([a, jnp.zeros((rows, 128 - width), F32)], axis=1)
        return a

    for hh in range(GLA_HEADS):
        o_ref[hh] = plane(hh * GLA_DK, GLA_DK).astype(o_ref.dtype)
        o_ref[GLA_HEADS + hh] = plane(GLA_HEADS * GLA_DK + hh * GLA_DK, GLA_DK).astype(o_ref.dtype)
        v_off = 2 * GLA_HEADS * GLA_DK + hh * GLA_DV
        o_ref[2 * GLA_HEADS + 2 * hh] = plane(v_off, 128).astype(o_ref.dtype)
        o_ref[2 * GLA_HEADS + 2 * hh + 1] = plane(v_off + 128, GLA_DV - 128).astype(o_ref.dtype)
    gl_ref[...] = plane(2 * GLA_HEADS * GLA_DK + GLA_HEADS * GLA_DV, GLA_RANK)


def _gla_proj(h, w, l, tm=512):
    m, d = h.shape
    n = W_GLA_COLS
    n_blk = 4 * GLA_HEADS
    return pl.pallas_call(
        _gla_proj_kernel,
        grid=(m // tm,),
        in_specs=[pl.BlockSpec((tm, d), lambda i: (i, 0)),
                  pl.BlockSpec((None, d, n), lambda i: (l, 0, 0))],
        out_specs=[pl.BlockSpec((n_blk, tm, 128), lambda i: (0, i, 0)),
                   pl.BlockSpec((tm, 128), lambda i: (i, 0))],
        out_shape=[jax.ShapeDtypeStruct((n_blk, m, 128), BF16),
                   jax.ShapeDtypeStruct((m, 128), F32)],
        compiler_params=_cparams(1),
        name="gla_proj",
    )(h, w)


def _sgu_tile(h_ref, w_ref, lng_ref, lnb_ref, ws_ref, bs_ref, o_ref):
    acc = jnp.dot(h_ref[...], w_ref[...], preferred_element_type=F32)
    u = jax.nn.gelu(acc[:, :SGU_WIDTH])
    gv = jax.nn.gelu(acc[:, SGU_WIDTH:])
    mu = jnp.mean(gv, axis=-1, keepdims=True)
    xc = gv - mu
    var = jnp.mean(xc * xc, axis=-1, keepdims=True)
    vh = (xc * lax.rsqrt(var + EPS) * lng_ref[...] + lnb_ref[...]).astype(BF16)
    half = 2 * SGU_GW
    lane = lax.broadcasted_iota(jnp.int32, (SGU_CHUNK, half), 1)
    first = lane < SGU_GW
    for c in range(h_ref.shape[0] // SGU_CHUNK):
        rows = slice(c * SGU_CHUNK, (c + 1) * SGU_CHUNK)
        for p in range(2):
            cols = slice(p * half, (p + 1) * half)
            vc = vh[rows, cols]
            r0 = jnp.dot(ws_ref[2 * p], vc, preferred_element_type=F32)
            r1 = jnp.dot(ws_ref[2 * p + 1], vc, preferred_element_type=F32)
            f = jnp.where(first, r0, r1) + bs_ref[:, cols]
            o_ref[rows, cols] = (u[rows, cols] * f).astype(o_ref.dtype)


def _t5_causal_bucket(dist):
    max_exact = REL_BUCKETS // 2
    d = np.maximum(dist, 1)
    large = max_exact + (np.log(d / max_exact) / np.log(REL_MAX_DIST / max_exact)
                         * (REL_BUCKETS - max_exact)).astype(np.int64)
    large = np.minimum(large, REL_BUCKETS - 1)
    return np.where(dist < max_exact, dist, large).astype(np.int32)


def _attn_bias_tiles(rel_bias):
    dist = np.arange(A_BLK + 1)
    period = 2 * A_BLK + 1
    tiles = []
    for gi, dilation in enumerate(DILATIONS):
        onehot = np.eye(REL_BUCKETS, dtype=np.float32)[_t5_causal_bucket(dist * dilation)]
        heads = rel_bias[:, gi * A_HPG:(gi + 1) * A_HPG].astype(F32)
        vec = jnp.dot(jnp.asarray(onehot), heads, precision=lax.Precision.HIGHEST).T
        seq = jnp.concatenate([vec[:, ::-1], jnp.full((A_HPG, period - A_BLK - 1), NEG, F32)], axis=1)
        flat = jnp.tile(seq, (1, A_BLK))[:, :A_BLK * 2 * A_BLK]
        tiles.append(flat.reshape(A_HPG, A_BLK, 2 * A_BLK))
    return jnp.stack(tiles, axis=0)


def _attn_unit(q, k, v, bias, scale, masked=None):
    s = lax.dot_general(q, k, (((1,), (1,)), ((), ())), preferred_element_type=F32) * scale + bias
    if masked is not None:
        s = jnp.where(masked, NEG, s)
    mx = jnp.max(s, axis=-1, keepdims=True)
    p = jnp.exp(s - mx)
    den = jnp.sum(p, axis=-1, keepdims=True)
    o = jnp.dot(p.astype(BF16), v, preferred_element_type=F32) / den
    return o, mx + jnp.log(den)


def _attn_sgu_kernel(*refs):
    groups = [refs[5 * gi:5 * gi + 5] for gi in range(N_A_GROUPS)]
    bias_ref = refs[15]
    sgu_in = refs[16:22]
    o_ref, oc_ref, og, lg = refs[22:]

    scale = A_HEAD_DIM ** -0.5
    col = lax.broadcasted_iota(jnp.int32, (A_BLK, 2 * A_BLK), 1)
    no_prev = jnp.logical_and(pl.program_id(2) == 0, col < A_BLK)
    for gi, (q_ref, k_ref, v_ref, kp_ref, vp_ref) in enumerate(groups):
        dil = DILATIONS[gi]
        n_blk = q_ref.shape[1] // A_BLK
        bias = bias_ref[gi]
        for r in range(dil):
            for n in range(n_blk):
                q = q_ref[r, n * A_BLK:(n + 1) * A_BLK, :]
                if n == 0:
                    k = jnp.concatenate([kp_ref[r], k_ref[r, 0:A_BLK, :]], axis=0)
                    v = jnp.concatenate([vp_ref[r], v_ref[r, 0:A_BLK, :]], axis=0)
                    o, lse = _attn_unit(q, k, v, bias, scale, masked=no_prev)
                else:
                    k = k_ref[r, (n - 1) * A_BLK:(n + 1) * A_BLK, :]
                    v = v_ref[r, (n - 1) * A_BLK:(n + 1) * A_BLK, :]
                    o, lse = _attn_unit(q, k, v, bias, scale)
                start = r + dil * A_BLK * n
                rows = pl.ds(start, A_BLK) if dil == 1 else pl.ds(start, A_BLK, stride=dil)
                og[gi, rows, :] = o
                lg[gi, rows, :] = jnp.broadcast_to(lse, (A_BLK, A_HEAD_DIM))
    _sgu_tile(*sgu_in, oc_ref)
    l0, l1, l2 = lg[0], lg[1], lg[2]
    mx = jnp.maximum(jnp.maximum(l0, l1), l2)
    w0, w1, w2 = jnp.exp(l0 - mx), jnp.exp(l1 - mx), jnp.exp(l2 - mx)
    o = (w0 * og[0] + w1 * og[1] + w2 * og[2]) / (w0 + w1 + w2)
    o_ref[...] = o.astype(o_ref.dtype)


def _attn_group_specs(dil, part):
    rows = part // dil
    blocks_per_part = rows // A_BLK

    def cur(which):
        return pl.BlockSpec((None, None, dil, rows, A_HEAD_DIM),
                            lambda b, hh, s: (which * A_HPG + hh, b, 0, s, 0))

    def prev(which):
        return pl.BlockSpec((None, None, dil, A_BLK, A_HEAD_DIM),
                            lambda b, hh, s: (which * A_HPG + hh, b, 0,
                                              jnp.maximum(s * blocks_per_part - 1, 0), 0))

    return [cur(0), cur(1), cur(2), prev(1), prev(2)]


def _attn_sgu(qkvs, bias, h, w, ln_g, ln_b, ws, bs_tile, l, batch, seq):
    m, d_model = h.shape
    part = seq // ATTN_SPLIT
    sgu_rows = m // (batch * A_HPG * ATTN_SPLIT)
    in_specs, args = [], []
    for gi, qkv in enumerate(qkvs):
        in_specs += _attn_group_specs(DILATIONS[gi], part)
        args += [qkv] * 5
    tile = lambda b, hh, s: (b * A_HPG + hh) * ATTN_SPLIT + s
    in_specs += [
        pl.BlockSpec((N_A_GROUPS, None, A_BLK, 2 * A_BLK), lambda b, hh, s: (0, hh, 0, 0)),
        pl.BlockSpec((sgu_rows, d_model), lambda b, hh, s: (tile(b, hh, s), 0)),
        pl.BlockSpec((None, d_model, 2 * SGU_WIDTH), lambda b, hh, s: (l, 0, 0),
                     pipeline_mode=pl.Buffered(1)),
        pl.BlockSpec((None, 1, SGU_WIDTH), lambda b, hh, s: (l, 0, 0)),
        pl.BlockSpec((None, 1, SGU_WIDTH), lambda b, hh, s: (l, 0, 0)),
        pl.BlockSpec((None, SGU_GROUPS, SGU_CHUNK, SGU_CHUNK), lambda b, hh, s: (l, 0, 0, 0)),
        pl.BlockSpec((None, SGU_CHUNK, SGU_WIDTH), lambda b, hh, s: (l, 0, 0))]
    args += [bias, h, w, ln_g, ln_b, ws, bs_tile]
    return pl.pallas_call(
        _attn_sgu_kernel,
        grid=(batch, A_HPG, ATTN_SPLIT),
        in_specs=in_specs,
        out_specs=[pl.BlockSpec((None, part, A_HEAD_DIM), lambda b, hh, s: (hh, b * ATTN_SPLIT + s, 0)),
                   pl.BlockSpec((sgu_rows, SGU_WIDTH), lambda b, hh, s: (tile(b, hh, s), 0))],
        out_shape=[jax.ShapeDtypeStruct((A_HPG, m, A_HEAD_DIM), BF16),
                   jax.ShapeDtypeStruct((m, SGU_WIDTH), BF16)],
        scratch_shapes=[pltpu.VMEM((N_A_GROUPS, part, A_HEAD_DIM), F32),
                        pltpu.VMEM((N_A_GROUPS, part, A_HEAD_DIM), F32)],
        compiler_params=_cparams(3),
        name="attn_sgu",
    )(*args)


def _gla_static_matrices(tile):
    t = np.arange(tile)
    same = (t[:, None] // GLA_CHUNK) == (t[None, :] // GLA_CHUNK)
    inc = same & (t[None, :] <= t[:, None])
    a = np.arange(GLA_ATILE)
    ti, si = a[:, None], a[None, :]
    masks = []
    for m in GLA_LEVELS:
        masks.append((ti // (2 * m) == si // (2 * m)) & ((ti // m) % 2 == 1) & ((si // m) % 2 == 0))
    return jnp.asarray(inc, BF16), jnp.asarray(np.stack(masks), F32)


def _gla_kernel(q_ref, k_ref, v_ref, gl_ref, wh_ref, wl_ref, bg_ref, gout_ref, inc_ref, lvl_ref,
                h_ref, wg_ref, o_ref, gates_ref, st_ref):
    @pl.when(pl.program_id(1) == 0)
    def _():
        st_ref[...] = jnp.zeros_like(st_ref)

    n_gate_chunks = gates_ref.shape[1] // GATE_CHUNK

    def gate_chunks(first, last):
        for c in range(first, last):
            cols = slice(c * GATE_CHUNK, (c + 1) * GATE_CHUNK)
            acc = jnp.dot(h_ref[...], wg_ref[:, cols], preferred_element_type=F32)
            gates_ref[:, cols] = jax.nn.sigmoid(acc).astype(gates_ref.dtype)

    t_rows = gl_ref.shape[0]
    n_sub = t_rows // 8
    width = GLA_HEADS * GLA_DK_PAD

    def split(x):
        hi = x.astype(BF16)
        return hi, (x - hi.astype(F32)).astype(BF16)

    g_hi, g_lo = split(gl_ref[...])
    wh, wl = wh_ref[...], wl_ref[...]
    x = (jnp.dot(g_hi, wh, preferred_element_type=F32) + jnp.dot(g_hi, wl, preferred_element_type=F32)
         + jnp.dot(g_lo, wh, preferred_element_type=F32) + bg_ref[...])
    gate_chunks(0, n_gate_chunks // 3)
    la =(jnp.minimum(x, 0.0) - jnp.log1p(jnp.exp(-jnp.abs(x)))) * (1.0 / GLA_TAU)
    la_hi, la_lo = split(la)
    inc = inc_ref[...]
    b = jnp.dot(inc, la_hi, preferred_element_type=F32) + jnp.dot(inc, la_lo, preferred_element_type=F32)
    gate_chunks(n_gate_chunks // 3, 2 * n_gate_chunks // 3)

    b3 = b.reshape(n_sub, 8, width)
    la3 = la.reshape(n_sub, 8, width)
    sub = lax.broadcasted_iota(jnp.int32, (1, 8, width), 1)

    def hi_step(y, bit):
        return jnp.where((sub & bit) == 0, pltpu.roll(y, 8 - bit, axis=1), y)

    def lo_step(y, bit):
        return jnp.where((sub & bit) != 0, pltpu.roll(y, bit, axis=1), y)

    def spread(y, groups, pick):
        y4 = y.reshape(n_sub // groups, groups, 8, width)
        return jnp.broadcast_to(y4[:, pick:pick + 1], y4.shape).reshape(n_sub, 8, width)

    hi = {1: b3}
    lo = {1: b3 - la3}
    for bit in (1, 2, 4):
        hi[2 * bit] = hi_step(hi[bit], bit)
        lo[2 * bit] = lo_step(lo[bit], bit)
    for groups in (2, 4, 8):
        hi[8 * groups] = spread(hi[8], groups, groups - 1)
        lo[8 * groups] = spread(lo[8], groups, 0)

    q_all = jnp.concatenate([q_ref[h] for h in range(GLA_HEADS)], axis=1).astype(F32) * (GLA_DK ** -0.5)
    k_all = jnp.concatenate([k_ref[h] for h in range(GLA_HEADS)], axis=1).astype(F32)
    q3 = q_all.reshape(n_sub, 8, width)
    k3 = k_all.reshape(n_sub, 8, width)

    def q_side(m):
        return (q3 * jnp.exp(b3 - lo[m])).reshape(t_rows, width).astype(BF16)

    def k_side(m):
        return (k3 * jnp.exp(hi[m] - b3)).reshape(t_rows, width).astype(BF16)

    qd = {m: q_side(m) for m in GLA_LEVELS + (GLA_CHUNK,)}
    kd = {m: (k_all.astype(BF16) if m == 1 else k_side(m)) for m in GLA_LEVELS + (GLA_CHUNK,)}
    diag = q_all * k_all
    v_heads = [jnp.concatenate([v_ref[2 * h], v_ref[2 * h + 1]], axis=1) for h in range(GLA_HEADS)]
    g_out = gout_ref[...]
    nt = (((1,), (1,)), ((), ()))
    chunks_per_atile = GLA_ATILE // GLA_CHUNK

    for a in range(t_rows // GLA_ATILE):
        if a == 1:
            gate_chunks(2 * n_gate_chunks // 3, n_gate_chunks)
        arows = slice(a * GLA_ATILE, (a + 1) * GLA_ATILE)
        for h in range(GLA_HEADS):
            cols = slice(h * GLA_DK_PAD, (h + 1) * GLA_DK_PAD)
            scores = None
            for li, m in enumerate(GLA_LEVELS):
                s = lax.dot_general(qd[m][arows, cols], kd[m][arows, cols], nt, preferred_element_type=F32)
                s = s * lvl_ref[li]
                scores = s if scores is None else scores + s
            v_a = v_heads[h][arows]
            rd = jnp.sum(diag[arows, cols], axis=-1, keepdims=True)
            intra = jnp.dot(scores.astype(BF16), v_a, preferred_element_type=F32) + rd * v_a.astype(F32)
            for cc in range(chunks_per_atile):
                c = a * chunks_per_atile + cc
                rows = slice(c * GLA_CHUNK, (c + 1) * GLA_CHUNK)
                st = st_ref[h]
                inter = lax.dot_general(qd[GLA_CHUNK][rows, cols], st.astype(BF16), nt,
                                        preferred_element_type=F32)
                o = intra[cc * GLA_CHUNK:(cc + 1) * GLA_CHUNK] + inter
                ms = jnp.sum(o * o, axis=-1, keepdims=True) * (1.0 / GLA_DV)
                o_n = (o * lax.rsqrt(ms + EPS) * g_out).astype(o_ref.dtype)
                o_ref[rows, h * GLA_DV:(h + 1) * GLA_DV] = o_n[:, :GLA_DV]
                last = c * GLA_CHUNK + GLA_CHUNK - 1
                decay = jnp.exp(b[last:last + 1, cols])
                upd = lax.dot_general(v_heads[h][rows], kd[GLA_CHUNK][rows, cols],
                                      (((0,), (0,)), ((), ())), preferred_element_type=F32)
                st_ref[h] = st * decay + upd


def _gla(glaqkv, gl, wup_hi, wup_lo, bg, gout, h, w_gates, l, batch, seq):
    m = gl.shape[0]
    d_model = h.shape[1]
    n_gates = N_GATE_COLS
    tile = GLA_TILE
    tiles_per_seq = seq // tile
    inc, lvl = _gla_static_matrices(tile)
    row = lambda b, t: b * tiles_per_seq + t
    width = GLA_HEADS * GLA_DK_PAD
    return pl.pallas_call(
        _gla_kernel,
        grid=(batch, tiles_per_seq),
        in_specs=[pl.BlockSpec((GLA_HEADS, tile, 128), lambda b, t: (0, row(b, t), 0)),
                  pl.BlockSpec((GLA_HEADS, tile, 128), lambda b, t: (1, row(b, t), 0)),
                  pl.BlockSpec((2 * GLA_HEADS, tile, 128), lambda b, t: (1, row(b, t), 0)),
                  pl.BlockSpec((tile, 128), lambda b, t: (row(b, t), 0)),
                  pl.BlockSpec((None, 128, width), lambda b, t: (l, 0, 0)),
                  pl.BlockSpec((None, 128, width), lambda b, t: (l, 0, 0)),
                  pl.BlockSpec((None, 1, width), lambda b, t: (l, 0, 0)),
                  pl.BlockSpec((None, 1, GLA_DV_PAD), lambda b, t: (l, 0, 0)),
                  pl.BlockSpec((tile, tile), lambda b, t: (0, 0)),
                  pl.BlockSpec((len(GLA_LEVELS), GLA_ATILE, GLA_ATILE), lambda b, t: (0, 0, 0)),
                  pl.BlockSpec((tile, d_model), lambda b, t: (row(b, t), 0)),
                  pl.BlockSpec((None, d_model, n_gates), lambda b, t: (l, 0, 0),
                               pipeline_mode=pl.Buffered(1))],
        out_specs=[pl.BlockSpec((tile, GLA_HEADS * GLA_DV), lambda b, t: (row(b, t), 0)),
                   pl.BlockSpec((tile, n_gates), lambda b, t: (row(b, t), 0))],
        out_shape=[jax.ShapeDtypeStruct((m, GLA_HEADS * GLA_DV), BF16),
                   jax.ShapeDtypeStruct((m, n_gates), BF16)],
        scratch_shapes=[pltpu.VMEM((GLA_HEADS, GLA_DV_PAD, GLA_DK_PAD), F32)],
        compiler_params=_cparams(2),
        name="gla_gates",
    )(glaqkv, glaqkv, glaqkv, gl, wup_hi, wup_lo, bg, gout, inc, lvl, h, w_gates)


def _merge_kernel(oa_ref, ob_ref, oc_ref, ga_ref, gb_ref, gc_ref, pa_ref, pb_ref, pc_ref, y_ref):
    oa = jnp.concatenate([oa_ref[hh] for hh in range(A_HPG)], axis=1)
    ya = jnp.dot(oa, pa_ref[...], preferred_element_type=F32)
    yb = jnp.dot(ob_ref[...], pb_ref[...], preferred_element_type=F32)
    yc = jnp.dot(oc_ref[...], pc_ref[...], preferred_element_type=F32)
    y = (ga_ref[...].astype(F32) * ya + gb_ref[...].astype(F32) * yb + gc_ref[...].astype(F32) * yc)
    y_ref[...] = y.astype(y_ref.dtype)


def _merge(oa, ob, oc, gates, pa, pb, pc, l, tm=256):
    m = ob.shape[0]
    n = pa.shape[2]
    const = lambda i: (l, 0, 0)
    return pl.pallas_call(
        _merge_kernel,
        grid=(m // tm,),
        in_specs=[pl.BlockSpec((A_HPG, tm, A_HEAD_DIM), lambda i: (0, i, 0)),
                  pl.BlockSpec((tm, ob.shape[1]), lambda i: (i, 0)),
                  pl.BlockSpec((tm, oc.shape[1]), lambda i: (i, 0)),
                  pl.BlockSpec((tm, n), lambda i: (i, 0)),
                  pl.BlockSpec((tm, n), lambda i: (i, 1)),
                  pl.BlockSpec((tm, n), lambda i: (i, 2)),
                  pl.BlockSpec((None,) + pa.shape[1:], const),
                  pl.BlockSpec((None,) + pb.shape[1:], const),
                  pl.BlockSpec((None,) + pc.shape[1:], const)],
        out_specs=pl.BlockSpec((tm, n), lambda i: (i, 0)),
        out_shape=jax.ShapeDtypeStruct((m, n), BF16),
        compiler_params=_cparams(1),
        name="merge",
    )(oa, ob, oc, gates, gates, gates, pa, pb, pc)


def _ffn_out_kernel(a_ref, w_ref, x_ref, o_ref):
    o_ref[...] = x_ref[...] + jnp.dot(a_ref[...], w_ref[...], preferred_element_type=F32)


def _ffn_out(a, w, x, l, tm=512):
    m, kdim = a.shape
    n = w.shape[2]
    return pl.pallas_call(
        _ffn_out_kernel,
        grid=(m // tm,),
        in_specs=[pl.BlockSpec((tm, kdim), lambda i: (i, 0)),
                  pl.BlockSpec((None, kdim, n), lambda i: (l, 0, 0), pipeline_mode=pl.Buffered(1)),
                  pl.BlockSpec((tm, n), lambda i: (i, 0))],
        out_specs=pl.BlockSpec((tm, n), lambda i: (i, 0)),
        out_shape=jax.ShapeDtypeStruct((m, n), F32),
        compiler_params=_cparams(1, VMEM_LIMIT_BIG_BYTES),
        name="ffn_out",
    )(a, w, x)


def _out_proj_kernel(y_ref, w_ref, x_ref, g_ref, o_ref, h_ref):
    x_new = x_ref[...] + jnp.dot(y_ref[...], w_ref[...], preferred_element_type=F32)
    o_ref[...] = x_new
    h_ref[...] = _rms(x_new, g_ref[...]).astype(h_ref.dtype)


def _out_proj(y, w, x, g, l, tm=512):
    m, d = x.shape
    return pl.pallas_call(
        _out_proj_kernel,
        grid=(m // tm,),
        in_specs=[pl.BlockSpec((tm, d), lambda i: (i, 0)),
                  pl.BlockSpec((None, d, d), lambda i: (l, 0, 0)),
                  pl.BlockSpec((tm, d), lambda i: (i, 0)),
                  pl.BlockSpec((None, 1, d), lambda i: (l, 0, 0))],
        out_specs=[pl.BlockSpec((tm, d), lambda i: (i, 0)),
                   pl.BlockSpec((tm, d), lambda i: (i, 0))],
        out_shape=[jax.ShapeDtypeStruct((m, d), F32),
                   jax.ShapeDtypeStruct((m, d), BF16)],
        compiler_params=_cparams(1),
        name="out_proj",
    )(y, w, x, g)


def _ffn_in_kernel(h_ref, wg_ref, wu_ref, o_ref):
    h = h_ref[...]
    g = jnp.dot(h, wg_ref[...], preferred_element_type=F32)
    u = jnp.dot(h, wu_ref[...], preferred_element_type=F32)
    o_ref[...] = (g * jax.nn.sigmoid(g) * u).astype(o_ref.dtype)


def _ffn_in(h, w, l, tm=2048, tn=512):
    m, d = h.shape
    f = w.shape[2] // 2
    up_off = f // tn
    return pl.pallas_call(
        _ffn_in_kernel,
        grid=(m // tm, f // tn),
        in_specs=[pl.BlockSpec((tm, d), lambda i, j: (i, 0)),
                  pl.BlockSpec((None, d, tn), lambda i, j: (l, 0, j)),
                  pl.BlockSpec((None, d, tn), lambda i, j: (l, 0, up_off + j))],
        out_specs=pl.BlockSpec((tm, tn), lambda i, j: (i, j)),
        out_shape=jax.ShapeDtypeStruct((m, f), BF16),
        compiler_params=_cparams(2),
        name="ffn_in",
    )(h, w, w)


def _pad_heads(w, off, n_heads, width, padded):
    lead = w.shape[:-1]
    cols = w[..., off:off + n_heads * width].reshape(lead + (n_heads, width))
    cols = jnp.pad(cols, [(0, 0)] * (len(lead) + 1) + [(0, padded - width)])
    return cols.reshape(lead + (n_heads * padded,))


def _prepare_params(w_in, gla_gate_up, gla_gate_b, gla_out_g, sgu_w, sgu_b, w_branch):
    depth = w_in.shape[0]
    w_in_b = w_in.astype(BF16)
    w_attn = w_in_b[:, :, _OFF_AQ:_OFF_BQ]
    n_gla = _OFF_CU - _OFF_BQ
    w_gla = jnp.pad(w_in_b[:, :, _OFF_BQ:_OFF_CU], ((0, 0), (0, 0), (0, W_GLA_COLS - n_gla)))
    w_sgu = w_in_b[:, :, _OFF_CU:_OFF_GATES]
    w_gates = w_in_b[:, :, _OFF_GATES:]
    wup = _pad_heads(gla_gate_up, 0, GLA_HEADS, GLA_DK, GLA_DK_PAD)
    wup = jnp.pad(wup, ((0, 0), (0, 128 - GLA_RANK), (0, 0)))
    wup_hi = wup.astype(BF16)
    wup_lo = (wup - wup_hi.astype(F32)).astype(BF16)
    bg = _pad_heads(gla_gate_b.reshape(depth, 1, -1), 0, GLA_HEADS, GLA_DK, GLA_DK_PAD)
    gout = jnp.pad(gla_out_g.reshape(depth, 1, GLA_DV), ((0, 0), (0, 0), (0, GLA_DV_PAD - GLA_DV)))
    tril = np.tril(np.ones((SGU_CHUNK, SGU_CHUNK), dtype=bool))
    ws = jnp.where(tril, sgu_w, 0.0).astype(BF16)
    bs_tile = jnp.repeat(jnp.swapaxes(sgu_b, 1, 2), SGU_GW, axis=2)
    pa = w_branch[:, :A_WIDTH].astype(BF16)
    pb = w_branch[:, A_WIDTH:A_WIDTH + GLA_HEADS * GLA_DV].astype(BF16)
    pc = w_branch[:, A_WIDTH + GLA_HEADS * GLA_DV:].astype(BF16)
    return dict(w_attn=w_attn, w_gla=w_gla, w_sgu=w_sgu, w_gates=w_gates, wup_hi=wup_hi, wup_lo=wup_lo,
                bg=bg, gout=gout, ws=ws, bs_tile=bs_tile, pa=pa, pb=pb, pc=pc)


def kernel(x, rel_bias, norm1_g, w_in, q_norm_g, k_norm_g, gla_gate_up, gla_gate_b, gla_out_g,
           sgu_ln_g, sgu_ln_b, sgu_w, sgu_b, w_branch, w_out, norm2_g, w_ffn_in, w_ffn_out):
    batch, seq, d = x.shape
    depth = w_in.shape[0]
    m = batch * seq
    assert seq % (A_BLK * max(DILATIONS)) == 0 and seq % GLA_TILE == 0 and d == D_MODEL
    p = _prepare_params(w_in, gla_gate_up, gla_gate_b, gla_out_g, sgu_w, sgu_b, w_branch)
    w_out_b = w_out.astype(BF16)
    w_ffn_in_b = w_ffn_in.astype(BF16)
    w_ffn_out_b = w_ffn_out.astype(BF16)
    n1 = norm1_g.reshape(depth, 1, d)
    n2 = norm2_g.reshape(depth, 1, d)
    gq = q_norm_g.reshape(depth, 1, A_HEAD_DIM)
    gk = k_norm_g.reshape(depth, 1, A_HEAD_DIM)
    ln_g = sgu_ln_g.reshape(depth, 1, SGU_WIDTH)
    ln_b = sgu_ln_b.reshape(depth, 1, SGU_WIDTH)
    bias = _attn_bias_tiles(rel_bias)
    xf = x.reshape(m, d)
    for l in range(depth):
        qkv0, h = _attn_proj_norm(xf, n1, p["w_attn"], gq, gk, l, batch, seq)
        qkvs = [qkv0] + [_attn_proj(h, p["w_attn"], gq, gk, l, gi, batch, seq) for gi in (1, 2)]
        glaqkv, gl = _gla_proj(h, p["w_gla"], l)
        o_a, o_c = _attn_sgu(qkvs, bias, h, p["w_sgu"], ln_g, ln_b, p["ws"], p["bs_tile"], l, batch, seq)
        o_b, gates = _gla(glaqkv, gl, p["wup_hi"], p["wup_lo"], p["bg"], p["gout"], h, p["w_gates"],
                          l, batch, seq)
        y = _merge(o_a, o_b, o_c, gates, p["pa"], p["pb"], p["pc"], l)
        xf, h2 = _out_proj(y, w_out_b, xf, n2, l)
        a = _ffn_in(h2, w_ffn_in_b, l)
        xf = _ffn_out(a, w_ffn_out_b, xf, l)
    return xf.reshape(batch, seq, d)
```

```python
import functools

import numpy as np
import jax
import jax.numpy as jnp
from jax import lax
from jax.experimental import pallas as pl
from jax.experimental.pallas import tpu as pltpu

F32 = jnp.float32
BF16 = jnp.bfloat16

D_MODEL = 2048
DILATIONS = (1, 4, 16)
N_A_GROUPS = 3
A_HEAD_DIM = 128
A_WIDTH = 512
A_HPG = 4
A_BLK = 128
GLA_HEADS = 4
GLA_DK = 96
GLA_DV = 192
GLA_DK_PAD = 128
GLA_DV_PAD = 256
GLA_RANK = 16
GLA_TAU = 16.0
GLA_CHUNK = 64
GLA_LEVELS = (1, 2, 4, 8, 16, 32)
GLA_ATILE = 128
SGU_WIDTH = 768
SGU_GROUPS = 4
SGU_GW = SGU_WIDTH // SGU_GROUPS
SGU_CHUNK = 128
D_FFN = 5632
REL_BUCKETS = 32
REL_MAX_DIST = 2048
EPS = 1e-6
NEG = -1e30

_OFF_AQ, _OFF_AK, _OFF_AV = 0, 1536, 3072
_OFF_BQ, _OFF_BK, _OFF_BV = 4608, 4992, 5376
_OFF_GL = 6144
_OFF_CU, _OFF_CV = 6160, 6928
_OFF_GATES = 7696

VMEM_LIMIT_BYTES = 56 * 1024 * 1024
VMEM_LIMIT_BIG_BYTES = 60 * 1024 * 1024

GLA_TILE = 256
NORM_CHUNK = 256
MERGE_CHUNK = 256
GATE_CHUNK = 1024
ATTN_SPLIT = 2
N_GATE_COLS = 3 * D_MODEL
W_GLA_COLS = 1664


def _cparams(n_axes, vmem_limit_bytes=VMEM_LIMIT_BYTES):
    return pltpu.CompilerParams(
        dimension_semantics=("arbitrary",) * n_axes,
        vmem_limit_bytes=vmem_limit_bytes,
    )


def _rms(x, g):
    ms = jnp.mean(x * x, axis=-1, keepdims=True)
    return x * lax.rsqrt(ms + EPS) * g


def _attn_proj_kernel(h_ref, wq_ref, wk_ref, wv_ref, gq_ref, gk_ref, o_ref, *scratch, dil):
    h = h_ref[...]
    sub_rows = h.shape[0] // dil
    for which, (w_ref, g_ref) in enumerate(((wq_ref, gq_ref), (wk_ref, gk_ref), (wv_ref, None))):
        acc = jnp.dot(h, w_ref[...], preferred_element_type=F32)
        for hh in range(A_HPG):
            p = which * A_HPG + hh
            a = acc[:, hh * A_HEAD_DIM:(hh + 1) * A_HEAD_DIM]
            if g_ref is not None:
                a = _rms(a, g_ref[...])
            if dil == 4:
                scr = scratch[0]
                scr[p] = a
                for r in range(dil):
                    o_ref[p, r] = scr[p, pl.ds(r, sub_rows, stride=dil), :].astype(o_ref.dtype)
            else:
                scr, scr2 = scratch
                scr[p] = a
                quarter = h.shape[0] // 4
                for r1 in range(4):
                    scr2[p, r1] = scr[p, pl.ds(r1, quarter, stride=4), :]
                    for r2 in range(4):
                        o_ref[p, r1 + 4 * r2] = scr2[p, r1, pl.ds(r2, sub_rows, stride=4), :].astype(o_ref.dtype)


def _attn_proj(h, w, gq, gk, l, gi, batch, seq, tm=1024):
    d_model = h.shape[1]
    dil = DILATIONS[gi]
    planes = 3 * A_HPG
    tiles_per_seq = seq // tm
    w_spec = lambda which: pl.BlockSpec((None, d_model, A_WIDTH),
                                        lambda b, t: (l, 0, which * N_A_GROUPS + gi))
    return pl.pallas_call(
        functools.partial(_attn_proj_kernel, dil=dil),
        grid=(batch, tiles_per_seq),
        in_specs=[pl.BlockSpec((tm, d_model), lambda b, t: (b * tiles_per_seq + t, 0)),
                  w_spec(0), w_spec(1), w_spec(2),
                  pl.BlockSpec((None, 1, A_HEAD_DIM), lambda b, t: (l, 0, 0)),
                  pl.BlockSpec((None, 1, A_HEAD_DIM), lambda b, t: (l, 0, 0))],
        out_specs=pl.BlockSpec((planes, None, dil, tm // dil, A_HEAD_DIM), lambda b, t: (0, b, 0, t, 0)),
        out_shape=jax.ShapeDtypeStruct((planes, batch, dil, seq // dil, A_HEAD_DIM), BF16),
        scratch_shapes=([pltpu.VMEM((planes, tm, A_HEAD_DIM), F32)]
                        + ([pltpu.VMEM((planes, 4, tm // 4, A_HEAD_DIM), F32)] if dil == 16 else [])),
        compiler_params=_cparams(2),
        name=f"attn_proj_g{gi}",
    )(h, w, w, w, gq, gk)


def _attn_proj_norm_kernel(x_ref, ng_ref, wq_ref, wk_ref, wv_ref, gq_ref, gk_ref, o_ref, h_ref):
    for c in range(x_ref.shape[0] // NORM_CHUNK):
        rows = slice(c * NORM_CHUNK, (c + 1) * NORM_CHUNK)
        h = _rms(x_ref[rows, :], ng_ref[...]).astype(h_ref.dtype)
        h_ref[rows, :] = h
        for which, (w_ref, g_ref) in enumerate(((wq_ref, gq_ref), (wk_ref, gk_ref), (wv_ref, None))):
            acc = jnp.dot(h, w_ref[...], preferred_element_type=F32)
            for hh in range(A_HPG):
                a = acc[:, hh * A_HEAD_DIM:(hh + 1) * A_HEAD_DIM]
                if g_ref is not None:
                    a = _rms(a, g_ref[...])
                o_ref[which * A_HPG + hh, 0, rows, :] = a.astype(o_ref.dtype)


def _attn_proj_norm(x, ng, w, gq, gk, l, batch, seq, tm=1024):
    m, d_model = x.shape
    planes = 3 * A_HPG
    tiles_per_seq = seq // tm
    w_spec = lambda which: pl.BlockSpec((None, d_model, A_WIDTH), lambda b, t: (l, 0, which * N_A_GROUPS))
    row = lambda b, t: b * tiles_per_seq + t
    return pl.pallas_call(
        _attn_proj_norm_kernel,
        grid=(batch, tiles_per_seq),
        in_specs=[pl.BlockSpec((tm, d_model), lambda b, t: (row(b, t), 0)),
                  pl.BlockSpec((None, 1, d_model), lambda b, t: (l, 0, 0)),
                  w_spec(0), w_spec(1), w_spec(2),
                  pl.BlockSpec((None, 1, A_HEAD_DIM), lambda b, t: (l, 0, 0)),
                  pl.BlockSpec((None, 1, A_HEAD_DIM), lambda b, t: (l, 0, 0))],
        out_specs=[pl.BlockSpec((planes, None, 1, tm, A_HEAD_DIM), lambda b, t: (0, b, 0, t, 0)),
                   pl.BlockSpec((tm, d_model), lambda b, t: (row(b, t), 0))],
        out_shape=[jax.ShapeDtypeStruct((planes, batch, 1, seq, A_HEAD_DIM), BF16),
                   jax.ShapeDtypeStruct((m, d_model), BF16)],
        compiler_params=_cparams(2),
        name="attn_proj_norm_g0",
    )(x, ng, w, w, w, gq, gk)


def _gla_proj_kernel(h_ref, w_ref, o_ref, gl_ref):
    acc = jnp.dot(h_ref[...], w_ref[...], preferred_element_type=F32)
    rows = acc.shape[0]

    def plane(off, width):
        a = acc[:, off:off + width]
        if width < 128:
            a = jnp.concatenate([a, jnp.zeros((rows, 128 - width), F32)], axis=1)
        return a

    for hh in range(GLA_HEADS):
        o_ref[hh] = plane(hh * GLA_DK, GLA_DK).astype(o_ref.dtype)
        o_ref[GLA_HEADS + hh] = plane(GLA_HEADS * GLA_DK + hh * GLA_DK, GLA_DK).astype(o_ref.dtype)
        v_off = 2 * GLA_HEADS * GLA_DK + hh * GLA_DV
        o_ref[2 * GLA_HEADS + 2 * hh] = plane(v_off, 128).astype(o_ref.dtype)
        o_ref[2 * GLA_HEADS + 2 * hh + 1] = plane(v_off + 128, GLA_DV - 128).astype(o_ref.dtype)
    gl_ref[...] = plane(2 * GLA_HEADS * GLA_DK + GLA_HEADS * GLA_DV, GLA_RANK)


def _gla_proj(h, w, l, tm=512):
    m, d = h.shape
    n = W_GLA_COLS
    n_blk = 4 * GLA_HEADS
    return pl.pallas_call(
        _gla_proj_kernel,
        grid=(m // tm,),
        in_specs=[pl.BlockSpec((tm, d), lambda i: (i, 0)),
                  pl.BlockSpec((None, d, n), lambda i: (l, 0, 0))],
        out_specs=[pl.BlockSpec((n_blk, tm, 128), lambda i: (0, i, 0)),
                   pl.BlockSpec((tm, 128), lambda i: (i, 0))],
        out_shape=[jax.ShapeDtypeStruct((n_blk, m, 128), BF16),
                   jax.ShapeDtypeStruct((m, 128), F32)],
        compiler_params=_cparams(1),
        name="gla_proj",
    )(h, w)


def _sgu_tile(h_ref, w_ref, lng_ref, lnb_ref, ws_ref, bs_ref, o_ref):
    acc = jnp.dot(h_ref[...], w_ref[...], preferred_element_type=F32)
    u = jax.nn.gelu(acc[:, :SGU_WIDTH])
    gv = jax.nn.gelu(acc[:, SGU_WIDTH:])
    mu = jnp.mean(gv, axis=-1, keepdims=True)
    xc = gv - mu
    var = jnp.mean(xc * xc, axis=-1, keepdims=True)
    vh = (xc * lax.rsqrt(var + EPS) * lng_ref[...] + lnb_ref[...]).astype(BF16)
    half = 2 * SGU_GW
    lane = lax.broadcasted_iota(jnp.int32, (SGU_CHUNK, half), 1)
    first = lane < SGU_GW
    for c in range(h_ref.shape[0] // SGU_CHUNK):
        rows = slice(c * SGU_CHUNK, (c + 1) * SGU_CHUNK)
        for p in range(2):
            cols = slice(p * half, (p + 1) * half)
            vc = vh[rows, cols]
            r0 = jnp.dot(ws_ref[2 * p], vc, preferred_element_type=F32)
            r1 = jnp.dot(ws_ref[2 * p + 1], vc, preferred_element_type=F32)
            f = jnp.where(first, r0, r1) + bs_ref[:, cols]
            o_ref[rows, cols] = (u[rows, cols] * f).astype(o_ref.dtype)


def _t5_causal_bucket(dist):
    max_exact = REL_BUCKETS // 2
    d = np.maximum(dist, 1)
    large = max_exact + (np.log(d / max_exact) / np.log(REL_MAX_DIST / max_exact)
                         * (REL_BUCKETS - max_exact)).astype(np.int64)
    large = np.minimum(large, REL_BUCKETS - 1)
    return np.where(dist < max_exact, dist, large).astype(np.int32)


def _attn_bias_tiles(rel_bias):
    dist = np.arange(A_BLK + 1)
    period = 2 * A_BLK + 1
    tiles = []
    for gi, dilation in enumerate(DILATIONS):
        onehot = np.eye(REL_BUCKETS, dtype=np.float32)[_t5_causal_bucket(dist * dilation)]
        heads = rel_bias[:, gi * A_HPG:(gi + 1) * A_HPG].astype(F32)
        vec = jnp.dot(jnp.asarray(onehot), heads, precision=lax.Precision.HIGHEST).T
        seq = jnp.concatenate([vec[:, ::-1], jnp.full((A_HPG, period - A_BLK - 1), NEG, F32)], axis=1)
        flat = jnp.tile(seq, (1, A_BLK))[:, :A_BLK * 2 * A_BLK]
        tiles.append(flat.reshape(A_HPG, A_BLK, 2 * A_BLK))
    return jnp.stack(tiles, axis=0)


def _attn_unit(q, k, v, bias, scale, masked=None):
    s = lax.dot_general(q, k, (((1,), (1,)), ((), ())), preferred_element_type=F32) * scale + bias
    if masked is not None:
        s = jnp.where(masked, NEG, s)
    mx = jnp.max(s, axis=-1, keepdims=True)
    p = jnp.exp(s - mx)
    den = jnp.sum(p, axis=-1, keepdims=True)
    o = jnp.dot(p.astype(BF16), v, preferred_element_type=F32) / den
    return o, mx + jnp.log(den)


def _attn_sgu_kernel(*refs):
    groups = [refs[5 * gi:5 * gi + 5] for gi in range(N_A_GROUPS)]
    bias_ref = refs[15]
    sgu_in = refs[16:22]
    o_ref, oc_ref, og, lg = refs[22:]

    scale = A_HEAD_DIM ** -0.5
    col = lax.broadcasted_iota(jnp.int32, (A_BLK, 2 * A_BLK), 1)
    no_prev = jnp.logical_and(pl.program_id(2) == 0, col < A_BLK)
    for gi, (q_ref, k_ref, v_ref, kp_ref, vp_ref) in enumerate(groups):
        dil = DILATIONS[gi]
        n_blk = q_ref.shape[1] // A_BLK
        bias = bias_ref[gi]
        for r in range(dil):
            for n in range(n_blk):
                q = q_ref[r, n * A_BLK:(n + 1) * A_BLK, :]
                if n == 0:
                    k = jnp.concatenate([kp_ref[r], k_ref[r, 0:A_BLK, :]], axis=0)
                    v = jnp.concatenate([vp_ref[r], v_ref[r, 0:A_BLK, :]], axis=0)
                    o, lse = _attn_unit(q, k, v, bias, scale, masked=no_prev)
                else:
                    k = k_ref[r, (n - 1) * A_BLK:(n + 1) * A_BLK, :]
                    v = v_ref[r, (n - 1) * A_BLK:(n + 1) * A_BLK, :]
                    o, lse = _attn_unit(q, k, v, bias, scale)
                start = r + dil * A_BLK * n
                rows = pl.ds(start, A_BLK) if dil == 1 else pl.ds(start, A_BLK, stride=dil)
                og[gi, rows, :] = o
                lg[gi, rows, :] = jnp.broadcast_to(lse, (A_BLK, A_HEAD_DIM))
    _sgu_tile(*sgu_in, oc_ref)
    l0, l1, l2 = lg[0], lg[1], lg[2]
    mx = jnp.maximum(jnp.maximum(l0, l1), l2)
    w0, w1, w2 = jnp.exp(l0 - mx), jnp.exp(l1 - mx), jnp.exp(l2 - mx)
    o = (w0 * og[0] + w1 * og[1] + w2 * og[2]) / (w0 + w1 + w2)
    o_ref[...] = o.astype(o_ref.dtype)


def _attn_group_specs(dil, part):
    rows = part // dil
    blocks_per_part = rows // A_BLK

    def cur(which):
        return pl.BlockSpec((None, None, dil, rows, A_HEAD_DIM),
                            lambda b, hh, s: (which * A_HPG + hh, b, 0, s, 0))

    def prev(which):
        return pl.BlockSpec((None, None, dil, A_BLK, A_HEAD_DIM),
                            lambda b, hh, s: (which * A_HPG + hh, b, 0,
                                              jnp.maximum(s * blocks_per_part - 1, 0), 0))

    return [cur(0), cur(1), cur(2), prev(1), prev(2)]


def _attn_sgu(qkvs, bias, h, w, ln_g, ln_b, ws, bs_tile, l, batch, seq):
    m, d_model = h.shape
    part = seq // ATTN_SPLIT
    sgu_rows = m // (batch * A_HPG * ATTN_SPLIT)
    in_specs, args = [], []
    for gi, qkv in enumerate(qkvs):
        in_specs += _attn_group_specs(DILATIONS[gi], part)
        args += [qkv] * 5
    tile = lambda b, hh, s: (b * A_HPG + hh) * ATTN_SPLIT + s
    in_specs += [
        pl.BlockSpec((N_A_GROUPS, None, A_BLK, 2 * A_BLK), lambda b, hh, s: (0, hh, 0, 0)),
        pl.BlockSpec((sgu_rows, d_model), lambda b, hh, s: (tile(b, hh, s), 0)),
        pl.BlockSpec((None, d_model, 2 * SGU_WIDTH), lambda b, hh, s: (l, 0, 0),
                     pipeline_mode=pl.Buffered(1)),
        pl.BlockSpec((None, 1, SGU_WIDTH), lambda b, hh, s: (l, 0, 0)),
        pl.BlockSpec((None, 1, SGU_WIDTH), lambda b, hh, s: (l, 0, 0)),
        pl.BlockSpec((None, SGU_GROUPS, SGU_CHUNK, SGU_CHUNK), lambda b, hh, s: (l, 0, 0, 0)),
        pl.BlockSpec((None, SGU_CHUNK, SGU_WIDTH), lambda b, hh, s: (l, 0, 0))]
    args += [bias, h, w, ln_g, ln_b, ws, bs_tile]
    return pl.pallas_call(
        _attn_sgu_kernel,
        grid=(batch, A_HPG, ATTN_SPLIT),
        in_specs=in_specs,
        out_specs=[pl.BlockSpec((None, part, A_HEAD_DIM), lambda b, hh, s: (hh, b * ATTN_SPLIT + s, 0)),
                   pl.BlockSpec((sgu_rows, SGU_WIDTH), lambda b, hh, s: (tile(b, hh, s), 0))],
        out_shape=[jax.ShapeDtypeStruct((A_HPG, m, A_HEAD_DIM), BF16),
                   jax.ShapeDtypeStruct((m, SGU_WIDTH), BF16)],
        scratch_shapes=[pltpu.VMEM((N_A_GROUPS, part, A_HEAD_DIM), F32),
                        pltpu.VMEM((N_A_GROUPS, part, A_HEAD_DIM), F32)],
        compiler_params=_cparams(3),
        name="attn_sgu",
    )(*args)


def _gla_static_matrices(tile):
    t = np.arange(tile)
    same = (t[:, None] // GLA_CHUNK) == (t[None, :] // GLA_CHUNK)
    inc = same & (t[None, :] <= t[:, None])
    a = np.arange(GLA_ATILE)
    ti, si = a[:, None], a[None, :]
    masks = []
    for m in GLA_LEVELS:
        masks.append((ti // (2 * m) == si // (2 * m)) & ((ti // m) % 2 == 1) & ((si // m) % 2 == 0))
    return jnp.asarray(inc, BF16), jnp.asarray(np.stack(masks), F32)


def _gla_kernel(q_ref, k_ref, v_ref, gl_ref, wh_ref, wl_ref, bg_ref, gout_ref, inc_ref, lvl_ref,
                h_ref, wg_ref, o_ref, gates_ref, st_ref):
    @pl.when(pl.program_id(1) == 0)
    def _():
        st_ref[...] = jnp.zeros_like(st_ref)

    n_gate_chunks = gates_ref.shape[1] // GATE_CHUNK

    def gate_chunks(first, last):
        for c in range(first, last):
            cols = slice(c * GATE_CHUNK, (c + 1) * GATE_CHUNK)
            acc = jnp.dot(h_ref[...], wg_ref[:, cols], preferred_element_type=F32)
            gates_ref[:, cols] = jax.nn.sigmoid(acc).astype(gates_ref.dtype)

    t_rows = gl_ref.shape[0]
    n_sub = t_rows // 8
    width = GLA_HEADS * GLA_DK_PAD

    def split(x):
        hi = x.astype(BF16)
        return hi, (x - hi.astype(F32)).astype(BF16)

    g_hi, g_lo = split(gl_ref[...])
    wh, wl = wh_ref[...], wl_ref[...]
    x = (jnp.dot(g_hi, wh, preferred_element_type=F32) + jnp.dot(g_hi, wl, preferred_element_type=F32)
         + jnp.dot(g_lo, wh, preferred_element_type=F32) + bg_ref[...])
    gate_chunks(0, n_gate_chunks // 3)
    la =(jnp.minimum(x, 0.0) - jnp.log1p(jnp.exp(-jnp.abs(x)))) * (1.0 / GLA_TAU)
    la_hi, la_lo = split(la)
    inc = inc_ref[...]
    b = jnp.dot(inc, la_hi, preferred_element_type=F32) + jnp.dot(inc, la_lo, preferred_element_type=F32)
    gate_chunks(n_gate_chunks // 3, 2 * n_gate_chunks // 3)

    b3 = b.reshape(n_sub, 8, width)
    la3 = la.reshape(n_sub, 8, width)
    sub = lax.broadcasted_iota(jnp.int32, (1, 8, width), 1)

    def hi_step(y, bit):
        return jnp.where((sub & bit) == 0, pltpu.roll(y, 8 - bit, axis=1), y)

    def lo_step(y, bit):
        return jnp.where((sub & bit) != 0, pltpu.roll(y, bit, axis=1), y)

    def spread(y, groups, pick):
        y4 = y.reshape(n_sub // groups, groups, 8, width)
        return jnp.broadcast_to(y4[:, pick:pick + 1], y4.shape).reshape(n_sub, 8, width)

    hi = {1: b3}
    lo = {1: b3 - la3}
    for bit in (1, 2, 4):
        hi[2 * bit] = hi_step(hi[bit], bit)
        lo[2 * bit] = lo_step(lo[bit], bit)
    for groups in (2, 4, 8):
        hi[8 * groups] = spread(hi[8], groups, groups - 1)
        lo[8 * groups] = spread(lo[8], groups, 0)

    q_all = jnp.concatenate([q_ref[h] for h in range(GLA_HEADS)], axis=1).astype(F32) * (GLA_DK ** -0.5)
    k_all = jnp.concatenate([k_ref[h] for h in range(GLA_HEADS)], axis=1).astype(F32)
    q3 = q_all.reshape(n_sub, 8, width)
    k3 = k_all.reshape(n_sub, 8, width)

    def q_side(m):
        return (q3 * jnp.exp(b3 - lo[m])).reshape(t_rows, width).astype(BF16)

    def k_side(m):
        return (k3 * jnp.exp(hi[m] - b3)).reshape(t_rows, width).astype(BF16)

    qd = {m: q_side(m) for m in GLA_LEVELS + (GLA_CHUNK,)}
    kd = {m: (k_all.astype(BF16) if m == 1 else k_side(m)) for m in GLA_LEVELS + (GLA_CHUNK,)}
    diag = q_all * k_all
    v_heads = [jnp.concatenate([v_ref[2 * h], v_ref[2 * h + 1]], axis=1) for h in range(GLA_HEADS)]
    g_out = gout_ref[...]
    nt = (((1,), (1,)), ((), ()))
    chunks_per_atile = GLA_ATILE // GLA_CHUNK

    for a in range(t_rows // GLA_ATILE):
        if a == 1:
            gate_chunks(2 * n_gate_chunks // 3, n_gate_chunks)
        arows = slice(a * GLA_ATILE, (a + 1) * GLA_ATILE)
        for h in range(GLA_HEADS):
            cols = slice(h * GLA_DK_PAD, (h + 1) * GLA_DK_PAD)
            scores = None
            for li, m in enumerate(GLA_LEVELS):
                s = lax.dot_general(qd[m][arows, cols], kd[m][arows, cols], nt, preferred_element_type=F32)
                s = s * lvl_ref[li]
                scores = s if scores is None else scores + s
            v_a = v_heads[h][arows]
            rd = jnp.sum(diag[arows, cols], axis=-1, keepdims=True)
            intra = jnp.dot(scores.astype(BF16), v_a, preferred_element_type=F32) + rd * v_a.astype(F32)
            for cc in range(chunks_per_atile):
                c = a * chunks_per_atile + cc
                rows = slice(c * GLA_CHUNK, (c + 1) * GLA_CHUNK)
                st = st_ref[h]
                inter = lax.dot_general(qd[GLA_CHUNK][rows, cols], st.astype(BF16), nt,
                                        preferred_element_type=F32)
                o = intra[cc * GLA_CHUNK:(cc + 1) * GLA_CHUNK] + inter
                ms = jnp.sum(o * o, axis=-1, keepdims=True) * (1.0 / GLA_DV)
                o_n = (o * lax.rsqrt(ms + EPS) * g_out).astype(o_ref.dtype)
                o_ref[rows, h * GLA_DV:(h + 1) * GLA_DV] = o_n[:, :GLA_DV]
                last = c * GLA_CHUNK + GLA_CHUNK - 1
                decay = jnp.exp(b[last:last + 1, cols])
                upd = lax.dot_general(v_heads[h][rows], kd[GLA_CHUNK][rows, cols],
                                      (((0,), (0,)), ((), ())), preferred_element_type=F32)
                st_ref[h] = st * decay + upd


def _gla(glaqkv, gl, wup_hi, wup_lo, bg, gout, h, w_gates, l, batch, seq):
    m = gl.shape[0]
    d_model = h.shape[1]
    n_gates = N_GATE_COLS
    tile = GLA_TILE
    tiles_per_seq = seq // tile
    inc, lvl = _gla_static_matrices(tile)
    row = lambda b, t: b * tiles_per_seq + t
    width = GLA_HEADS * GLA_DK_PAD
    return pl.pallas_call(
        _gla_kernel,
        grid=(batch, tiles_per_seq),
        in_specs=[pl.BlockSpec((GLA_HEADS, tile, 128), lambda b, t: (0, row(b, t), 0)),
                  pl.BlockSpec((GLA_HEADS, tile, 128), lambda b, t: (1, row(b, t), 0)),
                  pl.BlockSpec((2 * GLA_HEADS, tile, 128), lambda b, t: (1, row(b, t), 0)),
                  pl.BlockSpec((tile, 128), lambda b, t: (row(b, t), 0)),
                  pl.BlockSpec((None, 128, width), lambda b, t: (l, 0, 0)),
                  pl.BlockSpec((None, 128, width), lambda b, t: (l, 0, 0)),
                  pl.BlockSpec((None, 1, width), lambda b, t: (l, 0, 0)),
                  pl.BlockSpec((None, 1, GLA_DV_PAD), lambda b, t: (l, 0, 0)),
                  pl.BlockSpec((tile, tile), lambda b, t: (0, 0)),
                  pl.BlockSpec((len(GLA_LEVELS), GLA_ATILE, GLA_ATILE), lambda b, t: (0, 0, 0)),
                  pl.BlockSpec((tile, d_model), lambda b, t: (row(b, t), 0)),
                  pl.BlockSpec((None, d_model, n_gates), lambda b, t: (l, 0, 0),
                               pipeline_mode=pl.Buffered(1))],
        out_specs=[pl.BlockSpec((tile, GLA_HEADS * GLA_DV), lambda b, t: (row(b, t), 0)),
                   pl.BlockSpec((tile, n_gates), lambda b, t: (row(b, t), 0))],
        out_shape=[jax.ShapeDtypeStruct((m, GLA_HEADS * GLA_DV), BF16),
                   jax.ShapeDtypeStruct((m, n_gates), BF16)],
        scratch_shapes=[pltpu.VMEM((GLA_HEADS, GLA_DV_PAD, GLA_DK_PAD), F32)],
        compiler_params=_cparams(2),
        name="gla_gates",
    )(glaqkv, glaqkv, glaqkv, gl, wup_hi, wup_lo, bg, gout, inc, lvl, h, w_gates)


def _merge_out_kernel(oa_ref, ob_ref, oc_ref, ga_ref, gb_ref, gc_ref, pa_ref, pb_ref, pc_ref,
                      wo_ref, x_ref, g_ref, o_ref, h_ref):
    for c in range(oa_ref.shape[1] // MERGE_CHUNK):
        rows = slice(c * MERGE_CHUNK, (c + 1) * MERGE_CHUNK)
        oa = jnp.concatenate([oa_ref[hh, rows, :] for hh in range(A_HPG)], axis=1)
        ya = jnp.dot(oa, pa_ref[...], preferred_element_type=F32)
        yb = jnp.dot(ob_ref[rows, :], pb_ref[...], preferred_element_type=F32)
        yc = jnp.dot(oc_ref[rows, :], pc_ref[...], preferred_element_type=F32)
        y = (ga_ref[rows, :].astype(F32) * ya + gb_ref[rows, :].astype(F32) * yb
             + gc_ref[rows, :].astype(F32) * yc).astype(BF16)
        x_new = x_ref[rows, :] + jnp.dot(y, wo_ref[...], preferred_element_type=F32)
        o_ref[rows, :] = x_new
        h_ref[rows, :] = _rms(x_new, g_ref[...]).astype(h_ref.dtype)


def _merge_out(oa, ob, oc, gates, pa, pb, pc, w_out, x, g, l, tm=512):
    m, d = x.shape
    const = lambda i: (l, 0, 0)
    resident = lambda arr: pl.BlockSpec((None,) + arr.shape[1:], const, pipeline_mode=pl.Buffered(1))
    return pl.pallas_call(
        _merge_out_kernel,
        grid=(m // tm,),
        in_specs=[pl.BlockSpec((A_HPG, tm, A_HEAD_DIM), lambda i: (0, i, 0)),
                  pl.BlockSpec((tm, ob.shape[1]), lambda i: (i, 0)),
                  pl.BlockSpec((tm, oc.shape[1]), lambda i: (i, 0)),
                  pl.BlockSpec((tm, d), lambda i: (i, 0)),
                  pl.BlockSpec((tm, d), lambda i: (i, 1)),
                  pl.BlockSpec((tm, d), lambda i: (i, 2)),
                  resident(pa), resident(pb), resident(pc), resident(w_out),
                  pl.BlockSpec((tm, d), lambda i: (i, 0)),
                  pl.BlockSpec((None, 1, d), const)],
        out_specs=[pl.BlockSpec((tm, d), lambda i: (i, 0)),
                   pl.BlockSpec((tm, d), lambda i: (i, 0))],
        out_shape=[jax.ShapeDtypeStruct((m, d), F32),
                   jax.ShapeDtypeStruct((m, d), BF16)],
        compiler_params=_cparams(1, VMEM_LIMIT_BIG_BYTES),
        name="merge_out",
    )(oa, ob, oc, gates, gates, gates, pa, pb, pc, w_out, x, g)


def _ffn_out_kernel(a_ref, w_ref, x_ref, o_ref):
    o_ref[...] = x_ref[...] + jnp.dot(a_ref[...], w_ref[...], preferred_element_type=F32)


def _ffn_out(a, w, x, l, tm=512):
    m, kdim = a.shape
    n = w.shape[2]
    return pl.pallas_call(
        _ffn_out_kernel,
        grid=(m // tm,),
        in_specs=[pl.BlockSpec((tm, kdim), lambda i: (i, 0)),
                  pl.BlockSpec((None, kdim, n), lambda i: (l, 0, 0), pipeline_mode=pl.Buffered(1)),
                  pl.BlockSpec((tm, n), lambda i: (i, 0))],
        out_specs=pl.BlockSpec((tm, n), lambda i: (i, 0)),
        out_shape=jax.ShapeDtypeStruct((m, n), F32),
        compiler_params=_cparams(1, VMEM_LIMIT_BIG_BYTES),
        name="ffn_out",
    )(a, w, x)


def _ffn_in_kernel(h_ref, wg_ref, wu_ref, o_ref):
    h = h_ref[...]
    g = jnp.dot(h, wg_ref[...], preferred_element_type=F32)
    u = jnp.dot(h, wu_ref[...], preferred_element_type=F32)
    o_ref[...] = (g * jax.nn.sigmoid(g) * u).astype(o_ref.dtype)


def _ffn_in(h, w, l, tm=2048, tn=512):
    m, d = h.shape
    f = w.shape[2] // 2
    up_off = f // tn
    return pl.pallas_call(
        _ffn_in_kernel,
        grid=(m // tm, f // tn),
        in_specs=[pl.BlockSpec((tm, d), lambda i, j: (i, 0)),
                  pl.BlockSpec((None, d, tn), lambda i, j: (l, 0, j)),
                  pl.BlockSpec((None, d, tn), lambda i, j: (l, 0, up_off + j))],
        out_specs=pl.BlockSpec((tm, tn), lambda i, j: (i, j)),
        out_shape=jax.ShapeDtypeStruct((m, f), BF16),
        compiler_params=_cparams(2),
        name="ffn_in",
    )(h, w, w)


def _pad_heads(w, off, n_heads, width, padded):
    lead = w.shape[:-1]
    cols = w[..., off:off + n_heads * width].reshape(lead + (n_heads, width))
    cols = jnp.pad(cols, [(0, 0)] * (len(lead) + 1) + [(0, padded - width)])
    return cols.reshape(lead + (n_heads * padded,))


def _prepare_params(w_in, gla_gate_up, gla_gate_b, gla_out_g, sgu_w, sgu_b, w_branch):
    depth = w_in.shape[0]
    w_in_b = w_in.astype(BF16)
    w_attn = w_in_b[:, :, _OFF_AQ:_OFF_BQ]
    n_gla = _OFF_CU - _OFF_BQ
    w_gla = jnp.pad(w_in_b[:, :, _OFF_BQ:_OFF_CU], ((0, 0), (0, 0), (0, W_GLA_COLS - n_gla)))
    w_sgu = w_in_b[:, :, _OFF_CU:_OFF_GATES]
    w_gates = w_in_b[:, :, _OFF_GATES:]
    wup = _pad_heads(gla_gate_up, 0, GLA_HEADS, GLA_DK, GLA_DK_PAD)
    wup = jnp.pad(wup, ((0, 0), (0, 128 - GLA_RANK), (0, 0)))
    wup_hi = wup.astype(BF16)
    wup_lo = (wup - wup_hi.astype(F32)).astype(BF16)
    bg = _pad_heads(gla_gate_b.reshape(depth, 1, -1), 0, GLA_HEADS, GLA_DK, GLA_DK_PAD)
    gout = jnp.pad(gla_out_g.reshape(depth, 1, GLA_DV), ((0, 0), (0, 0), (0, GLA_DV_PAD - GLA_DV)))
    tril = np.tril(np.ones((SGU_CHUNK, SGU_CHUNK), dtype=bool))
    ws = jnp.where(tril, sgu_w, 0.0).astype(BF16)
    bs_tile = jnp.repeat(jnp.swapaxes(sgu_b, 1, 2), SGU_GW, axis=2)
    pa = w_branch[:, :A_WIDTH].astype(BF16)
    pb = w_branch[:, A_WIDTH:A_WIDTH + GLA_HEADS * GLA_DV].astype(BF16)
    pc = w_branch[:, A_WIDTH + GLA_HEADS * GLA_DV:].astype(BF16)
    return dict(w_attn=w_attn, w_gla=w_gla, w_sgu=w_sgu, w_gates=w_gates, wup_hi=wup_hi, wup_lo=wup_lo,
                bg=bg, gout=gout, ws=ws, bs_tile=bs_tile, pa=pa, pb=pb, pc=pc)


def kernel(x, rel_bias, norm1_g, w_in, q_norm_g, k_norm_g, gla_gate_up, gla_gate_b, gla_out_g,
           sgu_ln_g, sgu_ln_b, sgu_w, sgu_b, w_branch, w_out, norm2_g, w_ffn_in, w_ffn_out):
    batch, seq, d = x.shape
    depth = w_in.shape[0]
    m = batch * seq
    assert seq % (A_BLK * max(DILATIONS)) == 0 and seq % GLA_TILE == 0 and d == D_MODEL
    p = _prepare_params(w_in, gla_gate_up, gla_gate_b, gla_out_g, sgu_w, sgu_b, w_branch)
    w_out_b = w_out.astype(BF16)
    w_ffn_in_b = w_ffn_in.astype(BF16)
    w_ffn_out_b = w_ffn_out.astype(BF16)
    n1 = norm1_g.reshape(depth, 1, d)
    n2 = norm2_g.reshape(depth, 1, d)
    gq = q_norm_g.reshape(depth, 1, A_HEAD_DIM)
    gk = k_norm_g.reshape(depth, 1, A_HEAD_DIM)
    ln_g = sgu_ln_g.reshape(depth, 1, SGU_WIDTH)
    ln_b = sgu_ln_b.reshape(depth, 1, SGU_WIDTH)
    bias = _attn_bias_tiles(rel_bias)
    xf = x.reshape(m, d)
    for l in range(depth):
        qkv0, h = _attn_proj_norm(xf, n1, p["w_attn"], gq, gk, l, batch, seq)
        qkvs = [qkv0] + [_attn_proj(h, p["w_attn"], gq, gk, l, gi, batch, seq) for gi in (1, 2)]
        glaqkv, gl = _gla_proj(h, p["w_gla"], l)
        o_a, o_c = _attn_sgu(qkvs, bias, h, p["w_sgu"], ln_g, ln_b, p["ws"], p["bs_tile"], l, batch, seq)
        o_b, gates = _gla(glaqkv, gl, p["wup_hi"], p["wup_lo"], p["bg"], p["gout"], h, p["w_gates"],
                          l, batch, seq)
        xf, h2 = _merge_out(o_a, o_b, o_c, gates, p["pa"], p["pb"], p["pc"], w_out_b, xf, n2, l)
        a = _ffn_in(h2, w_ffn_in_b, l)
        xf = _ffn_out(a, w_ffn_out_b, xf, l)
    return xf.reshape(batch, seq, d)
```

```python
import functools

import numpy as np
import jax
import jax.numpy as jnp
from jax import lax
from jax.experimental import pallas as pl
from jax.experimental.pallas import tpu as pltpu

F32 = jnp.float32
BF16 = jnp.bfloat16

D_MODEL = 2048
DILATIONS = (1, 4, 16)
N_A_GROUPS = 3
A_HEAD_DIM = 128
A_WIDTH = 512
A_HPG = 4
A_BLK = 128
GLA_HEADS = 4
GLA_DK = 96
GLA_DV = 192
GLA_DK_PAD = 128
GLA_DV_PAD = 256
GLA_RANK = 16
GLA_TAU = 16.0
GLA_CHUNK = 64
GLA_LEVELS = (1, 2, 4, 8, 16, 32)
GLA_ATILE = 128
SGU_WIDTH = 768
SGU_GROUPS = 4
SGU_GW = SGU_WIDTH // SGU_GROUPS
SGU_CHUNK = 128
D_FFN = 5632
REL_BUCKETS = 32
REL_MAX_DIST = 2048
EPS = 1e-6
NEG = -1e30

_OFF_AQ, _OFF_AK, _OFF_AV = 0, 1536, 3072
_OFF_BQ, _OFF_BK, _OFF_BV = 4608, 4992, 5376
_OFF_GL = 6144
_OFF_CU, _OFF_CV = 6160, 6928
_OFF_GATES = 7696

VMEM_LIMIT_BYTES = 56 * 1024 * 1024
VMEM_LIMIT_BIG_BYTES = 60 * 1024 * 1024

GLA_TILE = 256
NORM_CHUNK = 256
MERGE_CHUNK = 256
FFN_CHUNK = 256
GATE_CHUNK = 1024
ATTN_SPLIT = 2
N_GATE_COLS = 3 * D_MODEL
W_GLA_COLS = 1664


def _cparams(n_axes, vmem_limit_bytes=VMEM_LIMIT_BYTES):
    return pltpu.CompilerParams(
        dimension_semantics=("arbitrary",) * n_axes,
        vmem_limit_bytes=vmem_limit_bytes,
    )


def _rms(x, g):
    ms = jnp.mean(x * x, axis=-1, keepdims=True)
    return x * lax.rsqrt(ms + EPS) * g


def _attn_proj_kernel(h_ref, wq_ref, wk_ref, wv_ref, gq_ref, gk_ref, o_ref, *scratch, dil):
    h = h_ref[...]
    sub_rows = h.shape[0] // dil
    for which, (w_ref, g_ref) in enumerate(((wq_ref, gq_ref), (wk_ref, gk_ref), (wv_ref, None))):
        acc = jnp.dot(h, w_ref[...], preferred_element_type=F32)
        for hh in range(A_HPG):
            p = which * A_HPG + hh
            a = acc[:, hh * A_HEAD_DIM:(hh + 1) * A_HEAD_DIM]
            if g_ref is not None:
                a = _rms(a, g_ref[...])
            if dil == 4:
                scr = scratch[0]
                scr[p] = a
                for r in range(dil):
                    o_ref[p, r] = scr[p, pl.ds(r, sub_rows, stride=dil), :].astype(o_ref.dtype)
            else:
                scr, scr2 = scratch
                scr[p] = a
                quarter = h.shape[0] // 4
                for r1 in range(4):
                    scr2[p, r1] = scr[p, pl.ds(r1, quarter, stride=4), :]
                    for r2 in range(4):
                        o_ref[p, r1 + 4 * r2] = scr2[p, r1, pl.ds(r2, sub_rows, stride=4), :].astype(o_ref.dtype)


def _attn_proj(h, w, gq, gk, l, gi, batch, seq, tm=1024):
    d_model = h.shape[1]
    dil = DILATIONS[gi]
    planes = 3 * A_HPG
    tiles_per_seq = seq // tm
    w_spec = lambda which: pl.BlockSpec((None, d_model, A_WIDTH),
                                        lambda b, t: (l, 0, which * N_A_GROUPS + gi))
    return pl.pallas_call(
        functools.partial(_attn_proj_kernel, dil=dil),
        grid=(batch, tiles_per_seq),
        in_specs=[pl.BlockSpec((tm, d_model), lambda b, t: (b * tiles_per_seq + t, 0)),
                  w_spec(0), w_spec(1), w_spec(2),
                  pl.BlockSpec((None, 1, A_HEAD_DIM), lambda b, t: (l, 0, 0)),
                  pl.BlockSpec((None, 1, A_HEAD_DIM), lambda b, t: (l, 0, 0))],
        out_specs=pl.BlockSpec((planes, None, dil, tm // dil, A_HEAD_DIM), lambda b, t: (0, b, 0, t, 0)),
        out_shape=jax.ShapeDtypeStruct((planes, batch, dil, seq // dil, A_HEAD_DIM), BF16),
        scratch_shapes=([pltpu.VMEM((planes, tm, A_HEAD_DIM), F32)]
                        + ([pltpu.VMEM((planes, 4, tm // 4, A_HEAD_DIM), F32)] if dil == 16 else [])),
        compiler_params=_cparams(2),
        name=f"attn_proj_g{gi}",
    )(h, w, w, w, gq, gk)


def _attn_proj_norm_kernel(x_ref, ng_ref, wq_ref, wk_ref, wv_ref, gq_ref, gk_ref, o_ref, h_ref):
    for c in range(x_ref.shape[0] // NORM_CHUNK):
        rows = slice(c * NORM_CHUNK, (c + 1) * NORM_CHUNK)
        h = _rms(x_ref[rows, :], ng_ref[...]).astype(h_ref.dtype)
        h_ref[rows, :] = h
        for which, (w_ref, g_ref) in enumerate(((wq_ref, gq_ref), (wk_ref, gk_ref), (wv_ref, None))):
            acc = jnp.dot(h, w_ref[...], preferred_element_type=F32)
            for hh in range(A_HPG):
                a = acc[:, hh * A_HEAD_DIM:(hh + 1) * A_HEAD_DIM]
                if g_ref is not None:
                    a = _rms(a, g_ref[...])
                o_ref[which * A_HPG + hh, 0, rows, :] = a.astype(o_ref.dtype)


def _attn_proj_norm(x, ng, w, gq, gk, l, batch, seq, tm=1024):
    m, d_model = x.shape
    planes = 3 * A_HPG
    tiles_per_seq = seq // tm
    w_spec = lambda which: pl.BlockSpec((None, d_model, A_WIDTH), lambda b, t: (l, 0, which * N_A_GROUPS))
    row = lambda b, t: b * tiles_per_seq + t
    return pl.pallas_call(
        _attn_proj_norm_kernel,
        grid=(batch, tiles_per_seq),
        in_specs=[pl.BlockSpec((tm, d_model), lambda b, t: (row(b, t), 0)),
                  pl.BlockSpec((None, 1, d_model), lambda b, t: (l, 0, 0)),
                  w_spec(0), w_spec(1), w_spec(2),
                  pl.BlockSpec((None, 1, A_HEAD_DIM), lambda b, t: (l, 0, 0)),
                  pl.BlockSpec((None, 1, A_HEAD_DIM), lambda b, t: (l, 0, 0))],
        out_specs=[pl.BlockSpec((planes, None, 1, tm, A_HEAD_DIM), lambda b, t: (0, b, 0, t, 0)),
                   pl.BlockSpec((tm, d_model), lambda b, t: (row(b, t), 0))],
        out_shape=[jax.ShapeDtypeStruct((planes, batch, 1, seq, A_HEAD_DIM), BF16),
                   jax.ShapeDtypeStruct((m, d_model), BF16)],
        compiler_params=_cparams(2),
        name="attn_proj_norm_g0",
    )(x, ng, w, w, w, gq, gk)


def _gla_proj_kernel(h_ref, w_ref, o_ref, gl_ref):
    for c in range(h_ref.shape[0] // NORM_CHUNK):
        rows = slice(c * NORM_CHUNK, (c + 1) * NORM_CHUNK)
        acc = jnp.dot(h_ref[rows, :], w_ref[...], preferred_element_type=F32)

        def plane(off, width):
            a = acc[:, off:off + width]
            if width < 128:
                a = jnp.concatenate([a, jnp.zeros((NORM_CHUNK, 128 - width), F32)], axis=1)
            return a

        for hh in range(GLA_HEADS):
            o_ref[hh, rows, :] = plane(hh * GLA_DK, GLA_DK).astype(o_ref.dtype)
            o_ref[GLA_HEADS + hh, rows, :] = plane(GLA_HEADS * GLA_DK + hh * GLA_DK, GLA_DK).astype(o_ref.dtype)
            v_off = 2 * GLA_HEADS * GLA_DK + hh * GLA_DV
            o_ref[2 * GLA_HEADS + 2 * hh, rows, :] = plane(v_off, 128).astype(o_ref.dtype)
            o_ref[2 * GLA_HEADS + 2 * hh + 1, rows, :] = plane(v_off + 128, GLA_DV - 128).astype(o_ref.dtype)
        gl_ref[rows, :] = plane(2 * GLA_HEADS * GLA_DK + GLA_HEADS * GLA_DV, GLA_RANK)


def _gla_proj(h, w, l, tm=512):
    m, d = h.shape
    n = W_GLA_COLS
    n_blk = 4 * GLA_HEADS
    return pl.pallas_call(
        _gla_proj_kernel,
        grid=(m // tm,),
        in_specs=[pl.BlockSpec((tm, d), lambda i: (i, 0)),
                  pl.BlockSpec((None, d, n), lambda i: (l, 0, 0))],
        out_specs=[pl.BlockSpec((n_blk, tm, 128), lambda i: (0, i, 0)),
                   pl.BlockSpec((tm, 128), lambda i: (i, 0))],
        out_shape=[jax.ShapeDtypeStruct((n_blk, m, 128), BF16),
                   jax.ShapeDtypeStruct((m, 128), F32)],
        compiler_params=_cparams(1),
        name="gla_proj",
    )(h, w)


def _sgu_tile(h_ref, w_ref, lng_ref, lnb_ref, ws_ref, bs_ref, o_ref):
    acc = jnp.dot(h_ref[...], w_ref[...], preferred_element_type=F32)
    u = jax.nn.gelu(acc[:, :SGU_WIDTH])
    gv = jax.nn.gelu(acc[:, SGU_WIDTH:])
    mu = jnp.mean(gv, axis=-1, keepdims=True)
    xc = gv - mu
    var = jnp.mean(xc * xc, axis=-1, keepdims=True)
    vh = (xc * lax.rsqrt(var + EPS) * lng_ref[...] + lnb_ref[...]).astype(BF16)
    half = 2 * SGU_GW
    lane = lax.broadcasted_iota(jnp.int32, (SGU_CHUNK, half), 1)
    first = lane < SGU_GW
    for c in range(h_ref.shape[0] // SGU_CHUNK):
        rows = slice(c * SGU_CHUNK, (c + 1) * SGU_CHUNK)
        for p in range(2):
            cols = slice(p * half, (p + 1) * half)
            vc = vh[rows, cols]
            r0 = jnp.dot(ws_ref[2 * p], vc, preferred_element_type=F32)
            r1 = jnp.dot(ws_ref[2 * p + 1], vc, preferred_element_type=F32)
            f = jnp.where(first, r0, r1) + bs_ref[:, cols]
            o_ref[rows, cols] = (u[rows, cols] * f).astype(o_ref.dtype)


def _t5_causal_bucket(dist):
    max_exact = REL_BUCKETS // 2
    d = np.maximum(dist, 1)
    large = max_exact + (np.log(d / max_exact) / np.log(REL_MAX_DIST / max_exact)
                         * (REL_BUCKETS - max_exact)).astype(np.int64)
    large = np.minimum(large, REL_BUCKETS - 1)
    return np.where(dist < max_exact, dist, large).astype(np.int32)


def _attn_bias_tiles(rel_bias):
    dist = np.arange(A_BLK + 1)
    period = 2 * A_BLK + 1
    tiles = []
    for gi, dilation in enumerate(DILATIONS):
        onehot = np.eye(REL_BUCKETS, dtype=np.float32)[_t5_causal_bucket(dist * dilation)]
        heads = rel_bias[:, gi * A_HPG:(gi + 1) * A_HPG].astype(F32)
        vec = jnp.dot(jnp.asarray(onehot), heads, precision=lax.Precision.HIGHEST).T
        seq = jnp.concatenate([vec[:, ::-1], jnp.full((A_HPG, period - A_BLK - 1), NEG, F32)], axis=1)
        flat = jnp.tile(seq, (1, A_BLK))[:, :A_BLK * 2 * A_BLK]
        tiles.append(flat.reshape(A_HPG, A_BLK, 2 * A_BLK))
    return jnp.stack(tiles, axis=0)


def _attn_unit(q, k, v, bias, scale, masked=None):
    s = lax.dot_general(q, k, (((1,), (1,)), ((), ())), preferred_element_type=F32) * scale + bias
    if masked is not None:
        s = jnp.where(masked, NEG, s)
    mx = jnp.max(s, axis=-1, keepdims=True)
    p = jnp.exp(s - mx)
    den = jnp.sum(p, axis=-1, keepdims=True)
    o = jnp.dot(p.astype(BF16), v, preferred_element_type=F32) / den
    return o, mx + jnp.log(den)


def _attn_sgu_kernel(*refs):
    groups = [refs[5 * gi:5 * gi + 5] for gi in range(N_A_GROUPS)]
    bias_ref = refs[15]
    sgu_in = refs[16:22]
    o_ref, oc_ref, og, lg = refs[22:]

    scale = A_HEAD_DIM ** -0.5
    col = lax.broadcasted_iota(jnp.int32, (A_BLK, 2 * A_BLK), 1)
    no_prev = jnp.logical_and(pl.program_id(2) == 0, col < A_BLK)
    for gi, (q_ref, k_ref, v_ref, kp_ref, vp_ref) in enumerate(groups):
        dil = DILATIONS[gi]
        n_blk = q_ref.shape[1] // A_BLK
        bias = bias_ref[gi]
        for r in range(dil):
            for n in range(n_blk):
                q = q_ref[r, n * A_BLK:(n + 1) * A_BLK, :]
                if n == 0:
                    k = jnp.concatenate([kp_ref[r], k_ref[r, 0:A_BLK, :]], axis=0)
                    v = jnp.concatenate([vp_ref[r], v_ref[r, 0:A_BLK, :]], axis=0)
                    o, lse = _attn_unit(q, k, v, bias, scale, masked=no_prev)
                else:
                    k = k_ref[r, (n - 1) * A_BLK:(n + 1) * A_BLK, :]
                    v = v_ref[r, (n - 1) * A_BLK:(n + 1) * A_BLK, :]
                    o, lse = _attn_unit(q, k, v, bias, scale)
                start = r + dil * A_BLK * n
                rows = pl.ds(start, A_BLK) if dil == 1 else pl.ds(start, A_BLK, stride=dil)
                og[gi, rows, :] = o
                lg[gi, rows, :] = jnp.broadcast_to(lse, (A_BLK, A_HEAD_DIM))
    _sgu_tile(*sgu_in, oc_ref)
    l0, l1, l2 = lg[0], lg[1], lg[2]
    mx = jnp.maximum(jnp.maximum(l0, l1), l2)
    w0, w1, w2 = jnp.exp(l0 - mx), jnp.exp(l1 - mx), jnp.exp(l2 - mx)
    o = (w0 * og[0] + w1 * og[1] + w2 * og[2]) / (w0 + w1 + w2)
    o_ref[...] = o.astype(o_ref.dtype)


def _attn_group_specs(dil, part):
    rows = part // dil
    blocks_per_part = rows // A_BLK

    def cur(which):
        return pl.BlockSpec((None, None, dil, rows, A_HEAD_DIM),
                            lambda b, hh, s: (which * A_HPG + hh, b, 0, s, 0))

    def prev(which):
        return pl.BlockSpec((None, None, dil, A_BLK, A_HEAD_DIM),
                            lambda b, hh, s: (which * A_HPG + hh, b, 0,
                                              jnp.maximum(s * blocks_per_part - 1, 0), 0))

    return [cur(0), cur(1), cur(2), prev(1), prev(2)]


def _attn_sgu(qkvs, bias, h, w, ln_g, ln_b, ws, bs_tile, l, batch, seq):
    m, d_model = h.shape
    part = seq // ATTN_SPLIT
    sgu_rows = m // (batch * A_HPG * ATTN_SPLIT)
    in_specs, args = [], []
    for gi, qkv in enumerate(qkvs):
        in_specs += _attn_group_specs(DILATIONS[gi], part)
        args += [qkv] * 5
    tile = lambda b, hh, s: (b * A_HPG + hh) * ATTN_SPLIT + s
    in_specs += [
        pl.BlockSpec((N_A_GROUPS, None, A_BLK, 2 * A_BLK), lambda b, hh, s: (0, hh, 0, 0)),
        pl.BlockSpec((sgu_rows, d_model), lambda b, hh, s: (tile(b, hh, s), 0)),
        pl.BlockSpec((None, d_model, 2 * SGU_WIDTH), lambda b, hh, s: (l, 0, 0),
                     pipeline_mode=pl.Buffered(1)),
        pl.BlockSpec((None, 1, SGU_WIDTH), lambda b, hh, s: (l, 0, 0)),
        pl.BlockSpec((None, 1, SGU_WIDTH), lambda b, hh, s: (l, 0, 0)),
        pl.BlockSpec((None, SGU_GROUPS, SGU_CHUNK, SGU_CHUNK), lambda b, hh, s: (l, 0, 0, 0)),
        pl.BlockSpec((None, SGU_CHUNK, SGU_WIDTH), lambda b, hh, s: (l, 0, 0))]
    args += [bias, h, w, ln_g, ln_b, ws, bs_tile]
    return pl.pallas_call(
        _attn_sgu_kernel,
        grid=(batch, A_HPG, ATTN_SPLIT),
        in_specs=in_specs,
        out_specs=[pl.BlockSpec((None, part, A_HEAD_DIM), lambda b, hh, s: (hh, b * ATTN_SPLIT + s, 0)),
                   pl.BlockSpec((sgu_rows, SGU_WIDTH), lambda b, hh, s: (tile(b, hh, s), 0))],
        out_shape=[jax.ShapeDtypeStruct((A_HPG, m, A_HEAD_DIM), BF16),
                   jax.ShapeDtypeStruct((m, SGU_WIDTH), BF16)],
        scratch_shapes=[pltpu.VMEM((N_A_GROUPS, part, A_HEAD_DIM), F32),
                        pltpu.VMEM((N_A_GROUPS, part, A_HEAD_DIM), F32)],
        compiler_params=_cparams(3),
        name="attn_sgu",
    )(*args)


def _gla_static_matrices(tile):
    t = np.arange(tile)
    same = (t[:, None] // GLA_CHUNK) == (t[None, :] // GLA_CHUNK)
    inc = same & (t[None, :] <= t[:, None])
    a = np.arange(GLA_ATILE)
    ti, si = a[:, None], a[None, :]
    masks = []
    for m in GLA_LEVELS:
        masks.append((ti // (2 * m) == si // (2 * m)) & ((ti // m) % 2 == 1) & ((si // m) % 2 == 0))
    return jnp.asarray(inc, BF16), jnp.asarray(np.stack(masks), F32)


def _gla_kernel(q_ref, k_ref, v_ref, gl_ref, wh_ref, wl_ref, bg_ref, gout_ref, inc_ref, lvl_ref,
                h_ref, wg_ref, o_ref, gates_ref, st_ref):
    @pl.when(pl.program_id(1) == 0)
    def _():
        st_ref[...] = jnp.zeros_like(st_ref)

    n_gate_chunks = gates_ref.shape[1] // GATE_CHUNK

    def gate_chunks(first, last):
        for c in range(first, last):
            cols = slice(c * GATE_CHUNK, (c + 1) * GATE_CHUNK)
            acc = jnp.dot(h_ref[...], wg_ref[:, cols], preferred_element_type=F32)
            gates_ref[:, cols] = jax.nn.sigmoid(acc).astype(gates_ref.dtype)

    t_rows = gl_ref.shape[0]
    n_sub = t_rows // 8
    width = GLA_HEADS * GLA_DK_PAD

    def split(x):
        hi = x.astype(BF16)
        return hi, (x - hi.astype(F32)).astype(BF16)

    g_hi, g_lo = split(gl_ref[...])
    wh, wl = wh_ref[...], wl_ref[...]
    x = (jnp.dot(g_hi, wh, preferred_element_type=F32) + jnp.dot(g_hi, wl, preferred_element_type=F32)
         + jnp.dot(g_lo, wh, preferred_element_type=F32) + bg_ref[...])
    gate_chunks(0, n_gate_chunks // 3)
    la =(jnp.minimum(x, 0.0) - jnp.log1p(jnp.exp(-jnp.abs(x)))) * (1.0 / GLA_TAU)
    la_hi, la_lo = split(la)
    inc = inc_ref[...]
    b = jnp.dot(inc, la_hi, preferred_element_type=F32) + jnp.dot(inc, la_lo, preferred_element_type=F32)
    gate_chunks(n_gate_chunks // 3, 2 * n_gate_chunks // 3)

    b3 = b.reshape(n_sub, 8, width)
    la3 = la.reshape(n_sub, 8, width)
    sub = lax.broadcasted_iota(jnp.int32, (1, 8, width), 1)

    def hi_step(y, bit):
        return jnp.where((sub & bit) == 0, pltpu.roll(y, 8 - bit, axis=1), y)

    def lo_step(y, bit):
        return jnp.where((sub & bit) != 0, pltpu.roll(y, bit, axis=1), y)

    def spread(y, groups, pick):
        y4 = y.reshape(n_sub // groups, groups, 8, width)
        return jnp.broadcast_to(y4[:, pick:pick + 1], y4.shape).reshape(n_sub, 8, width)

    hi = {1: b3}
    lo = {1: b3 - la3}
    for bit in (1, 2, 4):
        hi[2 * bit] = hi_step(hi[bit], bit)
        lo[2 * bit] = lo_step(lo[bit], bit)
    for groups in (2, 4, 8):
        hi[8 * groups] = spread(hi[8], groups, groups - 1)
        lo[8 * groups] = spread(lo[8], groups, 0)

    q_all = jnp.concatenate([q_ref[h] for h in range(GLA_HEADS)], axis=1).astype(F32) * (GLA_DK ** -0.5)
    k_all = jnp.concatenate([k_ref[h] for h in range(GLA_HEADS)], axis=1).astype(F32)
    q3 = q_all.reshape(n_sub, 8, width)
    k3 = k_all.reshape(n_sub, 8, width)

    def q_side(m):
        return (q3 * jnp.exp(b3 - lo[m])).reshape(t_rows, width).astype(BF16)

    def k_side(m):
        return (k3 * jnp.exp(hi[m] - b3)).reshape(t_rows, width).astype(BF16)

    qd = {m: q_side(m) for m in GLA_LEVELS + (GLA_CHUNK,)}
    kd = {m: (k_all.astype(BF16) if m == 1 else k_side(m)) for m in GLA_LEVELS + (GLA_CHUNK,)}
    diag = q_all * k_all
    v_heads = [jnp.concatenate([v_ref[2 * h], v_ref[2 * h + 1]], axis=1) for h in range(GLA_HEADS)]
    g_out = gout_ref[...]
    nt = (((1,), (1,)), ((), ()))
    chunks_per_atile = GLA_ATILE // GLA_CHUNK

    for a in range(t_rows // GLA_ATILE):
        if a == 1:
            gate_chunks(2 * n_gate_chunks // 3, n_gate_chunks)
        arows = slice(a * GLA_ATILE, (a + 1) * GLA_ATILE)
        for h in range(GLA_HEADS):
            cols = slice(h * GLA_DK_PAD, (h + 1) * GLA_DK_PAD)
            scores = None
            for li, m in enumerate(GLA_LEVELS):
                s = lax.dot_general(qd[m][arows, cols], kd[m][arows, cols], nt, preferred_element_type=F32)
                s = s * lvl_ref[li]
                scores = s if scores is None else scores + s
            v_a = v_heads[h][arows]
            rd = jnp.sum(diag[arows, cols], axis=-1, keepdims=True)
            intra = jnp.dot(scores.astype(BF16), v_a, preferred_element_type=F32) + rd * v_a.astype(F32)
            for cc in range(chunks_per_atile):
                c = a * chunks_per_atile + cc
                rows = slice(c * GLA_CHUNK, (c + 1) * GLA_CHUNK)
                st = st_ref[h]
                inter = lax.dot_general(qd[GLA_CHUNK][rows, cols], st.astype(BF16), nt,
                                        preferred_element_type=F32)
                o = intra[cc * GLA_CHUNK:(cc + 1) * GLA_CHUNK] + inter
                ms = jnp.sum(o * o, axis=-1, keepdims=True) * (1.0 / GLA_DV)
                o_n = (o * lax.rsqrt(ms + EPS) * g_out).astype(o_ref.dtype)
                o_ref[rows, h * GLA_DV:(h + 1) * GLA_DV] = o_n[:, :GLA_DV]
                last = c * GLA_CHUNK + GLA_CHUNK - 1
                decay = jnp.exp(b[last:last + 1, cols])
                upd = lax.dot_general(v_heads[h][rows], kd[GLA_CHUNK][rows, cols],
                                      (((0,), (0,)), ((), ())), preferred_element_type=F32)
                st_ref[h] = st * decay + upd


def _gla(glaqkv, gl, wup_hi, wup_lo, bg, gout, h, w_gates, l, batch, seq):
    m = gl.shape[0]
    d_model = h.shape[1]
    n_gates = N_GATE_COLS
    tile = GLA_TILE
    tiles_per_seq = seq // tile
    inc, lvl = _gla_static_matrices(tile)
    row = lambda b, t: b * tiles_per_seq + t
    width = GLA_HEADS * GLA_DK_PAD
    return pl.pallas_call(
        _gla_kernel,
        grid=(batch, tiles_per_seq),
        in_specs=[pl.BlockSpec((GLA_HEADS, tile, 128), lambda b, t: (0, row(b, t), 0)),
                  pl.BlockSpec((GLA_HEADS, tile, 128), lambda b, t: (1, row(b, t), 0)),
                  pl.BlockSpec((2 * GLA_HEADS, tile, 128), lambda b, t: (1, row(b, t), 0)),
                  pl.BlockSpec((tile, 128), lambda b, t: (row(b, t), 0)),
                  pl.BlockSpec((None, 128, width), lambda b, t: (l, 0, 0)),
                  pl.BlockSpec((None, 128, width), lambda b, t: (l, 0, 0)),
                  pl.BlockSpec((None, 1, width), lambda b, t: (l, 0, 0)),
                  pl.BlockSpec((None, 1, GLA_DV_PAD), lambda b, t: (l, 0, 0)),
                  pl.BlockSpec((tile, tile), lambda b, t: (0, 0)),
                  pl.BlockSpec((len(GLA_LEVELS), GLA_ATILE, GLA_ATILE), lambda b, t: (0, 0, 0)),
                  pl.BlockSpec((tile, d_model), lambda b, t: (row(b, t), 0)),
                  pl.BlockSpec((None, d_model, n_gates), lambda b, t: (l, 0, 0),
                               pipeline_mode=pl.Buffered(1))],
        out_specs=[pl.BlockSpec((tile, GLA_HEADS * GLA_DV), lambda b, t: (row(b, t), 0)),
                   pl.BlockSpec((tile, n_gates), lambda b, t: (row(b, t), 0))],
        out_shape=[jax.ShapeDtypeStruct((m, GLA_HEADS * GLA_DV), BF16),
                   jax.ShapeDtypeStruct((m, n_gates), BF16)],
        scratch_shapes=[pltpu.VMEM((GLA_HEADS, GLA_DV_PAD, GLA_DK_PAD), F32)],
        compiler_params=_cparams(2),
        name="gla_gates",
    )(glaqkv, glaqkv, glaqkv, gl, wup_hi, wup_lo, bg, gout, inc, lvl, h, w_gates)


def _merge_out_kernel(oa_ref, ob_ref, oc_ref, ga_ref, gb_ref, gc_ref, pa_ref, pb_ref, pc_ref,
                      wo_ref, x_ref, g_ref, o_ref, h_ref):
    for c in range(oa_ref.shape[1] // MERGE_CHUNK):
        rows = slice(c * MERGE_CHUNK, (c + 1) * MERGE_CHUNK)
        oa = jnp.concatenate([oa_ref[hh, rows, :] for hh in range(A_HPG)], axis=1)
        ya = jnp.dot(oa, pa_ref[...], preferred_element_type=F32)
        yb = jnp.dot(ob_ref[rows, :], pb_ref[...], preferred_element_type=F32)
        yc = jnp.dot(oc_ref[rows, :], pc_ref[...], preferred_element_type=F32)
        y = (ga_ref[rows, :].astype(F32) * ya + gb_ref[rows, :].astype(F32) * yb
             + gc_ref[rows, :].astype(F32) * yc).astype(BF16)
        x_new = x_ref[rows, :] + jnp.dot(y, wo_ref[...], preferred_element_type=F32)
        o_ref[rows, :] = x_new
        h_ref[rows, :] = _rms(x_new, g_ref[...]).astype(h_ref.dtype)


def _merge_out(oa, ob, oc, gates, pa, pb, pc, w_out, x, g, l, tm=512):
    m, d = x.shape
    const = lambda i: (l, 0, 0)
    resident = lambda arr: pl.BlockSpec((None,) + arr.shape[1:], const, pipeline_mode=pl.Buffered(1))
    return pl.pallas_call(
        _merge_out_kernel,
        grid=(m // tm,),
        in_specs=[pl.BlockSpec((A_HPG, tm, A_HEAD_DIM), lambda i: (0, i, 0)),
                  pl.BlockSpec((tm, ob.shape[1]), lambda i: (i, 0)),
                  pl.BlockSpec((tm, oc.shape[1]), lambda i: (i, 0)),
                  pl.BlockSpec((tm, d), lambda i: (i, 0)),
                  pl.BlockSpec((tm, d), lambda i: (i, 1)),
                  pl.BlockSpec((tm, d), lambda i: (i, 2)),
                  resident(pa), resident(pb), resident(pc), resident(w_out),
                  pl.BlockSpec((tm, d), lambda i: (i, 0)),
                  pl.BlockSpec((None, 1, d), const)],
        out_specs=[pl.BlockSpec((tm, d), lambda i: (i, 0)),
                   pl.BlockSpec((tm, d), lambda i: (i, 0))],
        out_shape=[jax.ShapeDtypeStruct((m, d), F32),
                   jax.ShapeDtypeStruct((m, d), BF16)],
        compiler_params=_cparams(1, VMEM_LIMIT_BIG_BYTES),
        name="merge_out",
    )(oa, ob, oc, gates, gates, gates, pa, pb, pc, w_out, x, g)


def _ffn_out_kernel(a_ref, w_ref, x_ref, o_ref):
    o_ref[...] = x_ref[...] + jnp.dot(a_ref[...], w_ref[...], preferred_element_type=F32)


def _ffn_out(a, w, x, l, tm=512):
    m, kdim = a.shape
    n = w.shape[2]
    return pl.pallas_call(
        _ffn_out_kernel,
        grid=(m // tm,),
        in_specs=[pl.BlockSpec((tm, kdim), lambda i: (i, 0)),
                  pl.BlockSpec((None, kdim, n), lambda i: (l, 0, 0), pipeline_mode=pl.Buffered(1)),
                  pl.BlockSpec((tm, n), lambda i: (i, 0))],
        out_specs=pl.BlockSpec((tm, n), lambda i: (i, 0)),
        out_shape=jax.ShapeDtypeStruct((m, n), F32),
        compiler_params=_cparams(1, VMEM_LIMIT_BIG_BYTES),
        name="ffn_out",
    )(a, w, x)


def _ffn_in_kernel(h_ref, wg_ref, wu_ref, o_ref):
    for c in range(h_ref.shape[0] // FFN_CHUNK):
        rows = slice(c * FFN_CHUNK, (c + 1) * FFN_CHUNK)
        h = h_ref[rows, :]
        g = jnp.dot(h, wg_ref[...], preferred_element_type=F32)
        u = jnp.dot(h, wu_ref[...], preferred_element_type=F32)
        o_ref[rows, :] = (g * jax.nn.sigmoid(g) * u).astype(o_ref.dtype)


def _ffn_in(h, w, l, tm=2048, tn=512):
    m, d = h.shape
    f = w.shape[2] // 2
    up_off = f // tn
    return pl.pallas_call(
        _ffn_in_kernel,
        grid=(m // tm, f // tn),
        in_specs=[pl.BlockSpec((tm, d), lambda i, j: (i, 0)),
                  pl.BlockSpec((None, d, tn), lambda i, j: (l, 0, j)),
                  pl.BlockSpec((None, d, tn), lambda i, j: (l, 0, up_off + j))],
        out_specs=pl.BlockSpec((tm, tn), lambda i, j: (i, j)),
        out_shape=jax.ShapeDtypeStruct((m, f), BF16),
        compiler_params=_cparams(2),
        name="ffn_in",
    )(h, w, w)


def _pad_heads(w, off, n_heads, width, padded):
    lead = w.shape[:-1]
    cols = w[..., off:off + n_heads * width].reshape(lead + (n_heads, width))
    cols = jnp.pad(cols, [(0, 0)] * (len(lead) + 1) + [(0, padded - width)])
    return cols.reshape(lead + (n_heads * padded,))


def _prepare_params(w_in, gla_gate_up, gla_gate_b, gla_out_g, sgu_w, sgu_b, w_branch):
    depth = w_in.shape[0]
    w_in_b = w_in.astype(BF16)
    w_attn = w_in_b[:, :, _OFF_AQ:_OFF_BQ]
    n_gla = _OFF_CU - _OFF_BQ
    w_gla = jnp.pad(w_in_b[:, :, _OFF_BQ:_OFF_CU], ((0, 0), (0, 0), (0, W_GLA_COLS - n_gla)))
    w_sgu = w_in_b[:, :, _OFF_CU:_OFF_GATES]
    w_gates = w_in_b[:, :, _OFF_GATES:]
    wup = _pad_heads(gla_gate_up, 0, GLA_HEADS, GLA_DK, GLA_DK_PAD)
    wup = jnp.pad(wup, ((0, 0), (0, 128 - GLA_RANK), (0, 0)))
    wup_hi = wup.astype(BF16)
    wup_lo = (wup - wup_hi.astype(F32)).astype(BF16)
    bg = _pad_heads(gla_gate_b.reshape(depth, 1, -1), 0, GLA_HEADS, GLA_DK, GLA_DK_PAD)
    gout = jnp.pad(gla_out_g.reshape(depth, 1, GLA_DV), ((0, 0), (0, 0), (0, GLA_DV_PAD - GLA_DV)))
    tril = np.tril(np.ones((SGU_CHUNK, SGU_CHUNK), dtype=bool))
    ws = jnp.where(tril, sgu_w, 0.0).astype(BF16)
    bs_tile = jnp.repeat(jnp.swapaxes(sgu_b, 1, 2), SGU_GW, axis=2)
    pa = w_branch[:, :A_WIDTH].astype(BF16)
    pb = w_branch[:, A_WIDTH:A_WIDTH + GLA_HEADS * GLA_DV].astype(BF16)
    pc = w_branch[:, A_WIDTH + GLA_HEADS * GLA_DV:].astype(BF16)
    return dict(w_attn=w_attn, w_gla=w_gla, w_sgu=w_sgu, w_gates=w_gates, wup_hi=wup_hi, wup_lo=wup_lo,
                bg=bg, gout=gout, ws=ws, bs_tile=bs_tile, pa=pa, pb=pb, pc=pc)


def kernel(x, rel_bias, norm1_g, w_in, q_norm_g, k_norm_g, gla_gate_up, gla_gate_b, gla_out_g,
           sgu_ln_g, sgu_ln_b, sgu_w, sgu_b, w_branch, w_out, norm2_g, w_ffn_in, w_ffn_out):
    batch, seq, d = x.shape
    depth = w_in.shape[0]
    m = batch * seq
    assert seq % (A_BLK * max(DILATIONS)) == 0 and seq % GLA_TILE == 0 and d == D_MODEL
    p = _prepare_params(w_in, gla_gate_up, gla_gate_b, gla_out_g, sgu_w, sgu_b, w_branch)
    w_out_b = w_out.astype(BF16)
    w_ffn_in_b = w_ffn_in.astype(BF16)
    w_ffn_out_b = w_ffn_out.astype(BF16)
    n1 = norm1_g.reshape(depth, 1, d)
    n2 = norm2_g.reshape(depth, 1, d)
    gq = q_norm_g.reshape(depth, 1, A_HEAD_DIM)
    gk = k_norm_g.reshape(depth, 1, A_HEAD_DIM)
    ln_g = sgu_ln_g.reshape(depth, 1, SGU_WIDTH)
    ln_b = sgu_ln_b.reshape(depth, 1, SGU_WIDTH)
    bias = _attn_bias_tiles(rel_bias)
    xf = x.reshape(m, d)
    for l in range(depth):
        qkv0, h = _attn_proj_norm(xf, n1, p["w_attn"], gq, gk, l, batch, seq)
        qkvs = [qkv0] + [_attn_proj(h, p["w_attn"], gq, gk, l, gi, batch, seq) for gi in (1, 2)]
        glaqkv, gl = _gla_proj(h, p["w_gla"], l)
        o_a, o_c = _attn_sgu(qkvs, bias, h, p["w_sgu"], ln_g, ln_b, p["ws"], p["bs_tile"], l, batch, seq)
        o_b, gates = _gla(glaqkv, gl, p["wup_hi"], p["wup_lo"], p["bg"], p["gout"], h, p["w_gates"],
                          l, batch, seq)
        xf, h2 = _merge_out(o_a, o_b, o_c, gates, p["pa"], p["pb"], p["pc"], w_out_b, xf, n2, l)
        a = _ffn_in(h2, w_ffn_in_b, l)
        xf = _ffn_out(a, w_ffn_out_b, xf, l)
    return xf.reshape(batch, seq, d)
```

```python
import functools

import numpy as np
import jax
import jax.numpy as jnp
from jax import lax
from jax.experimental import pallas as pl
from jax.experimental.pallas import tpu as pltpu

F32 = jnp.float32
BF16 = jnp.bfloat16

D_MODEL = 2048
DILATIONS = (1, 4, 16)
N_A_GROUPS = 3
A_HEAD_DIM = 128
A_WIDTH = 512
A_HPG = 4
A_BLK = 128
GLA_HEADS = 4
GLA_DK = 96
GLA_DV = 192
GLA_DK_PAD = 128
GLA_DV_PAD = 256
GLA_RANK = 16
GLA_TAU = 16.0
GLA_CHUNK = 64
GLA_LEVELS = (1, 2, 4, 8, 16, 32)
GLA_ATILE = 128
SGU_WIDTH = 768
SGU_GROUPS = 4
SGU_GW = SGU_WIDTH // SGU_GROUPS
SGU_CHUNK = 128
D_FFN = 5632
REL_BUCKETS = 32
REL_MAX_DIST = 2048
EPS = 1e-6
NEG = -1e30

_OFF_AQ, _OFF_AK, _OFF_AV = 0, 1536, 3072
_OFF_BQ, _OFF_BK, _OFF_BV = 4608, 4992, 5376
_OFF_GL = 6144
_OFF_CU, _OFF_CV = 6160, 6928
_OFF_GATES = 7696

VMEM_LIMIT_BYTES = 56 * 1024 * 1024
VMEM_LIMIT_BIG_BYTES = 60 * 1024 * 1024

GLA_TILE = 256
NORM_CHUNK = 256
MERGE_CHUNK = 256
FFN_CHUNK = 256
GATE_CHUNK = 1024
ATTN_SPLIT = 2
N_GATE_COLS = 3 * D_MODEL
W_GLA_COLS = 1664


def _cparams(n_axes, vmem_limit_bytes=VMEM_LIMIT_BYTES):
    return pltpu.CompilerParams(
        dimension_semantics=("arbitrary",) * n_axes,
        vmem_limit_bytes=vmem_limit_bytes,
    )


def _rms(x, g):
    ms = jnp.mean(x * x, axis=-1, keepdims=True)
    return x * lax.rsqrt(ms + EPS) * g


def _attn_proj_kernel(h_ref, wq_ref, wk_ref, wv_ref, gq_ref, gk_ref, o_ref, *scratch, dil):
    h = h_ref[...]
    sub_rows = h.shape[0] // dil
    for which, (w_ref, g_ref) in enumerate(((wq_ref, gq_ref), (wk_ref, gk_ref), (wv_ref, None))):
        acc = jnp.dot(h, w_ref[...], preferred_element_type=F32)
        for hh in range(A_HPG):
            p = which * A_HPG + hh
            a = acc[:, hh * A_HEAD_DIM:(hh + 1) * A_HEAD_DIM]
            if g_ref is not None:
                a = _rms(a, g_ref[...])
            if dil == 4:
                scr = scratch[0]
                scr[p] = a
                for r in range(dil):
                    o_ref[p, r] = scr[p, pl.ds(r, sub_rows, stride=dil), :].astype(o_ref.dtype)
            else:
                scr, scr2 = scratch
                scr[p] = a
                quarter = h.shape[0] // 4
                for r1 in range(4):
                    scr2[p, r1] = scr[p, pl.ds(r1, quarter, stride=4), :]
                    for r2 in range(4):
                        o_ref[p, r1 + 4 * r2] = scr2[p, r1, pl.ds(r2, sub_rows, stride=4), :].astype(o_ref.dtype)


def _attn_proj(h, w, gq, gk, l, gi, batch, seq, tm=1024):
    d_model = h.shape[1]
    dil = DILATIONS[gi]
    planes = 3 * A_HPG
    tiles_per_seq = seq // tm
    w_spec = lambda which: pl.BlockSpec((None, d_model, A_WIDTH),
                                        lambda b, t: (l, 0, which * N_A_GROUPS + gi))
    return pl.pallas_call(
        functools.partial(_attn_proj_kernel, dil=dil),
        grid=(batch, tiles_per_seq),
        in_specs=[pl.BlockSpec((tm, d_model), lambda b, t: (b * tiles_per_seq + t, 0)),
                  w_spec(0), w_spec(1), w_spec(2),
                  pl.BlockSpec((None, 1, A_HEAD_DIM), lambda b, t: (l, 0, 0)),
                  pl.BlockSpec((None, 1, A_HEAD_DIM), lambda b, t: (l, 0, 0))],
        out_specs=pl.BlockSpec((planes, None, dil, tm // dil, A_HEAD_DIM), lambda b, t: (0, b, 0, t, 0)),
        out_shape=jax.ShapeDtypeStruct((planes, batch, dil, seq // dil, A_HEAD_DIM), BF16),
        scratch_shapes=([pltpu.VMEM((planes, tm, A_HEAD_DIM), F32)]
                        + ([pltpu.VMEM((planes, 4, tm // 4, A_HEAD_DIM), F32)] if dil == 16 else [])),
        compiler_params=_cparams(2),
        name=f"attn_proj_g{gi}",
    )(h, w, w, w, gq, gk)


def _attn_proj_norm_kernel(x_ref, ng_ref, wq_ref, wk_ref, wv_ref, gq_ref, gk_ref, o_ref, h_ref):
    for c in range(x_ref.shape[0] // NORM_CHUNK):
        rows = slice(c * NORM_CHUNK, (c + 1) * NORM_CHUNK)
        h = _rms(x_ref[rows, :], ng_ref[...]).astype(h_ref.dtype)
        h_ref[rows, :] = h
        for which, (w_ref, g_ref) in enumerate(((wq_ref, gq_ref), (wk_ref, gk_ref), (wv_ref, None))):
            acc = jnp.dot(h, w_ref[...], preferred_element_type=F32)
            for hh in range(A_HPG):
                a = acc[:, hh * A_HEAD_DIM:(hh + 1) * A_HEAD_DIM]
                if g_ref is not None:
                    a = _rms(a, g_ref[...])
                o_ref[which * A_HPG + hh, 0, rows, :] = a.astype(o_ref.dtype)


def _attn_proj_norm(x, ng, w, gq, gk, l, batch, seq, tm=1024):
    m, d_model = x.shape
    planes = 3 * A_HPG
    tiles_per_seq = seq // tm
    w_spec = lambda which: pl.BlockSpec((None, d_model, A_WIDTH), lambda b, t: (l, 0, which * N_A_GROUPS))
    row = lambda b, t: b * tiles_per_seq + t
    return pl.pallas_call(
        _attn_proj_norm_kernel,
        grid=(batch, tiles_per_seq),
        in_specs=[pl.BlockSpec((tm, d_model), lambda b, t: (row(b, t), 0)),
                  pl.BlockSpec((None, 1, d_model), lambda b, t: (l, 0, 0)),
                  w_spec(0), w_spec(1), w_spec(2),
                  pl.BlockSpec((None, 1, A_HEAD_DIM), lambda b, t: (l, 0, 0)),
                  pl.BlockSpec((None, 1, A_HEAD_DIM), lambda b, t: (l, 0, 0))],
        out_specs=[pl.BlockSpec((planes, None, 1, tm, A_HEAD_DIM), lambda b, t: (0, b, 0, t, 0)),
                   pl.BlockSpec((tm, d_model), lambda b, t: (row(b, t), 0))],
        out_shape=[jax.ShapeDtypeStruct((planes, batch, 1, seq, A_HEAD_DIM), BF16),
                   jax.ShapeDtypeStruct((m, d_model), BF16)],
        compiler_params=_cparams(2),
        name="attn_proj_norm_g0",
    )(x, ng, w, w, w, gq, gk)


def _gla_proj_kernel(h_ref, w_ref, o_ref, gl_ref):
    for c in range(h_ref.shape[0] // NORM_CHUNK):
        rows = slice(c * NORM_CHUNK, (c + 1) * NORM_CHUNK)
        acc = jnp.dot(h_ref[rows, :], w_ref[...], preferred_element_type=F32)

        def plane(off, width):
            a = acc[:, off:off + width]
            if width < 128:
                a = jnp.concatenate([a, jnp.zeros((NORM_CHUNK, 128 - width), F32)], axis=1)
            return a

        for hh in range(GLA_HEADS):
            o_ref[hh, rows, :] = plane(hh * GLA_DK, GLA_DK).astype(o_ref.dtype)
            o_ref[GLA_HEADS + hh, rows, :] = plane(GLA_HEADS * GLA_DK + hh * GLA_DK, GLA_DK).astype(o_ref.dtype)
            v_off = 2 * GLA_HEADS * GLA_DK + hh * GLA_DV
            o_ref[2 * GLA_HEADS + 2 * hh, rows, :] = plane(v_off, 128).astype(o_ref.dtype)
            o_ref[2 * GLA_HEADS + 2 * hh + 1, rows, :] = plane(v_off + 128, GLA_DV - 128).astype(o_ref.dtype)
        gl_ref[rows, :] = plane(2 * GLA_HEADS * GLA_DK + GLA_HEADS * GLA_DV, GLA_RANK)


def _gla_proj(h, w, l, tm=512):
    m, d = h.shape
    n = W_GLA_COLS
    n_blk = 4 * GLA_HEADS
    return pl.pallas_call(
        _gla_proj_kernel,
        grid=(m // tm,),
        in_specs=[pl.BlockSpec((tm, d), lambda i: (i, 0)),
                  pl.BlockSpec((None, d, n), lambda i: (l, 0, 0))],
        out_specs=[pl.BlockSpec((n_blk, tm, 128), lambda i: (0, i, 0)),
                   pl.BlockSpec((tm, 128), lambda i: (i, 0))],
        out_shape=[jax.ShapeDtypeStruct((n_blk, m, 128), BF16),
                   jax.ShapeDtypeStruct((m, 128), F32)],
        compiler_params=_cparams(1),
        name="gla_proj",
    )(h, w)


def _sgu_tile(h_ref, w_ref, lng_ref, lnb_ref, ws_ref, bs_ref, o_ref):
    acc = jnp.dot(h_ref[...], w_ref[...], preferred_element_type=F32)
    u = jax.nn.gelu(acc[:, :SGU_WIDTH])
    gv = jax.nn.gelu(acc[:, SGU_WIDTH:])
    mu = jnp.mean(gv, axis=-1, keepdims=True)
    xc = gv - mu
    var = jnp.mean(xc * xc, axis=-1, keepdims=True)
    vh = (xc * lax.rsqrt(var + EPS) * lng_ref[...] + lnb_ref[...]).astype(BF16)
    half = 2 * SGU_GW
    lane = lax.broadcasted_iota(jnp.int32, (SGU_CHUNK, half), 1)
    first = lane < SGU_GW
    for c in range(h_ref.shape[0] // SGU_CHUNK):
        rows = slice(c * SGU_CHUNK, (c + 1) * SGU_CHUNK)
        for p in range(2):
            cols = slice(p * half, (p + 1) * half)
            vc = vh[rows, cols]
            r0 = jnp.dot(ws_ref[2 * p], vc, preferred_element_type=F32)
            r1 = jnp.dot(ws_ref[2 * p + 1], vc, preferred_element_type=F32)
            f = jnp.where(first, r0, r1) + bs_ref[:, cols]
            o_ref[rows, cols] = (u[rows, cols] * f).astype(o_ref.dtype)


def _t5_causal_bucket(dist):
    max_exact = REL_BUCKETS // 2
    d = np.maximum(dist, 1)
    large = max_exact + (np.log(d / max_exact) / np.log(REL_MAX_DIST / max_exact)
                         * (REL_BUCKETS - max_exact)).astype(np.int64)
    large = np.minimum(large, REL_BUCKETS - 1)
    return np.where(dist < max_exact, dist, large).astype(np.int32)


def _attn_bias_tiles(rel_bias):
    dist = np.arange(A_BLK + 1)
    period = 2 * A_BLK + 1
    tiles = []
    for gi, dilation in enumerate(DILATIONS):
        onehot = np.eye(REL_BUCKETS, dtype=np.float32)[_t5_causal_bucket(dist * dilation)]
        heads = rel_bias[:, gi * A_HPG:(gi + 1) * A_HPG].astype(F32)
        vec = jnp.dot(jnp.asarray(onehot), heads, precision=lax.Precision.HIGHEST).T
        seq = jnp.concatenate([vec[:, ::-1], jnp.full((A_HPG, period - A_BLK - 1), NEG, F32)], axis=1)
        flat = jnp.tile(seq, (1, A_BLK))[:, :A_BLK * 2 * A_BLK]
        tiles.append(flat.reshape(A_HPG, A_BLK, 2 * A_BLK))
    return jnp.stack(tiles, axis=0)


def _attn_unit(q, k, v, bias, scale, masked=None):
    s = lax.dot_general(q, k, (((1,), (1,)), ((), ())), preferred_element_type=F32) * scale + bias
    if masked is not None:
        s = jnp.where(masked, NEG, s)
    mx = jnp.max(s, axis=-1, keepdims=True)
    p = jnp.exp(s - mx)
    den = jnp.sum(p, axis=-1, keepdims=True)
    o = jnp.dot(p.astype(BF16), v, preferred_element_type=F32) / den
    return o, mx + jnp.log(den)


def _attn_sgu_kernel(*refs):
    groups = [refs[5 * gi:5 * gi + 5] for gi in range(N_A_GROUPS)]
    bias_ref = refs[15]
    sgu_in = refs[16:22]
    o_ref, oc_ref, og, lg = refs[22:]

    scale = A_HEAD_DIM ** -0.5
    col = lax.broadcasted_iota(jnp.int32, (A_BLK, 2 * A_BLK), 1)
    no_prev = jnp.logical_and(pl.program_id(2) == 0, col < A_BLK)
    for gi, (q_ref, k_ref, v_ref, kp_ref, vp_ref) in enumerate(groups):
        dil = DILATIONS[gi]
        n_blk = q_ref.shape[1] // A_BLK
        bias = bias_ref[gi]
        for r in range(dil):
            for n in range(n_blk):
                q = q_ref[r, n * A_BLK:(n + 1) * A_BLK, :]
                if n == 0:
                    k = jnp.concatenate([kp_ref[r], k_ref[r, 0:A_BLK, :]], axis=0)
                    v = jnp.concatenate([vp_ref[r], v_ref[r, 0:A_BLK, :]], axis=0)
                    o, lse = _attn_unit(q, k, v, bias, scale, masked=no_prev)
                else:
                    k = k_ref[r, (n - 1) * A_BLK:(n + 1) * A_BLK, :]
                    v = v_ref[r, (n - 1) * A_BLK:(n + 1) * A_BLK, :]
                    o, lse = _attn_unit(q, k, v, bias, scale)
                start = r + dil * A_BLK * n
                rows = pl.ds(start, A_BLK) if dil == 1 else pl.ds(start, A_BLK, stride=dil)
                og[gi, rows, :] = o
                lg[gi, rows, :] = jnp.broadcast_to(lse, (A_BLK, A_HEAD_DIM))
    _sgu_tile(*sgu_in, oc_ref)
    l0, l1, l2 = lg[0], lg[1], lg[2]
    mx = jnp.maximum(jnp.maximum(l0, l1), l2)
    w0, w1, w2 = jnp.exp(l0 - mx), jnp.exp(l1 - mx), jnp.exp(l2 - mx)
    o = (w0 * og[0] + w1 * og[1] + w2 * og[2]) / (w0 + w1 + w2)
    o_ref[...] = o.astype(o_ref.dtype)


def _attn_group_specs(dil, part):
    rows = part // dil
    blocks_per_part = rows // A_BLK

    def cur(which):
        return pl.BlockSpec((None, None, dil, rows, A_HEAD_DIM),
                            lambda b, hh, s: (which * A_HPG + hh, b, 0, s, 0))

    def prev(which):
        return pl.BlockSpec((None, None, dil, A_BLK, A_HEAD_DIM),
                            lambda b, hh, s: (which * A_HPG + hh, b, 0,
                                              jnp.maximum(s * blocks_per_part - 1, 0), 0))

    return [cur(0), cur(1), cur(2), prev(1), prev(2)]


def _attn_sgu(qkvs, bias, h, w, ln_g, ln_b, ws, bs_tile, l, batch, seq):
    m, d_model = h.shape
    part = seq // ATTN_SPLIT
    sgu_rows = m // (batch * A_HPG * ATTN_SPLIT)
    in_specs, args = [], []
    for gi, qkv in enumerate(qkvs):
        in_specs += _attn_group_specs(DILATIONS[gi], part)
        args += [qkv] * 5
    tile = lambda b, hh, s: (b * A_HPG + hh) * ATTN_SPLIT + s
    in_specs += [
        pl.BlockSpec((N_A_GROUPS, None, A_BLK, 2 * A_BLK), lambda b, hh, s: (0, hh, 0, 0)),
        pl.BlockSpec((sgu_rows, d_model), lambda b, hh, s: (tile(b, hh, s), 0)),
        pl.BlockSpec((None, d_model, 2 * SGU_WIDTH), lambda b, hh, s: (l, 0, 0),
                     pipeline_mode=pl.Buffered(1)),
        pl.BlockSpec((None, 1, SGU_WIDTH), lambda b, hh, s: (l, 0, 0)),
        pl.BlockSpec((None, 1, SGU_WIDTH), lambda b, hh, s: (l, 0, 0)),
        pl.BlockSpec((None, SGU_GROUPS, SGU_CHUNK, SGU_CHUNK), lambda b, hh, s: (l, 0, 0, 0)),
        pl.BlockSpec((None, SGU_CHUNK, SGU_WIDTH), lambda b, hh, s: (l, 0, 0))]
    args += [bias, h, w, ln_g, ln_b, ws, bs_tile]
    return pl.pallas_call(
        _attn_sgu_kernel,
        grid=(batch, A_HPG, ATTN_SPLIT),
        in_specs=in_specs,
        out_specs=[pl.BlockSpec((None, part, A_HEAD_DIM), lambda b, hh, s: (hh, b * ATTN_SPLIT + s, 0)),
                   pl.BlockSpec((sgu_rows, SGU_WIDTH), lambda b, hh, s: (tile(b, hh, s), 0))],
        out_shape=[jax.ShapeDtypeStruct((A_HPG, m, A_HEAD_DIM), BF16),
                   jax.ShapeDtypeStruct((m, SGU_WIDTH), BF16)],
        scratch_shapes=[pltpu.VMEM((N_A_GROUPS, part, A_HEAD_DIM), F32),
                        pltpu.VMEM((N_A_GROUPS, part, A_HEAD_DIM), F32)],
        compiler_params=_cparams(3),
        name="attn_sgu",
    )(*args)


def _gla_static_matrices(tile):
    t = np.arange(tile)
    same = (t[:, None] // GLA_CHUNK) == (t[None, :] // GLA_CHUNK)
    inc = same & (t[None, :] <= t[:, None])
    a = np.arange(GLA_ATILE)
    ti, si = a[:, None], a[None, :]
    masks = []
    for m in GLA_LEVELS:
        masks.append((ti // (2 * m) == si // (2 * m)) & ((ti // m) % 2 == 1) & ((si // m) % 2 == 0))
    return jnp.asarray(inc, BF16), jnp.asarray(np.stack(masks), F32)


def _gla_kernel(q_ref, k_ref, v_ref, gl_ref, wh_ref, wl_ref, bg_ref, gout_ref, inc_ref, lvl_ref,
                h_ref, wg_ref, o_ref, gates_ref, st_ref):
    @pl.when(pl.program_id(1) == 0)
    def _():
        st_ref[...] = jnp.zeros_like(st_ref)

    n_gate_chunks = gates_ref.shape[1] // GATE_CHUNK

    def gate_chunks(first, last):
        for c in range(first, last):
            cols = slice(c * GATE_CHUNK, (c + 1) * GATE_CHUNK)
            acc = jnp.dot(h_ref[...], wg_ref[:, cols], preferred_element_type=F32)
            gates_ref[:, cols] = jax.nn.sigmoid(acc).astype(gates_ref.dtype)

    t_rows = gl_ref.shape[0]
    n_sub = t_rows // 8
    width = GLA_HEADS * GLA_DK_PAD

    def split(x):
        hi = x.astype(BF16)
        return hi, (x - hi.astype(F32)).astype(BF16)

    g_hi, g_lo = split(gl_ref[...])
    wh, wl = wh_ref[...], wl_ref[...]
    x = (jnp.dot(g_hi, wh, preferred_element_type=F32) + jnp.dot(g_hi, wl, preferred_element_type=F32)
         + jnp.dot(g_lo, wh, preferred_element_type=F32) + bg_ref[...])
    gate_chunks(0, n_gate_chunks // 3)
    la =(jnp.minimum(x, 0.0) - jnp.log1p(jnp.exp(-jnp.abs(x)))) * (1.0 / GLA_TAU)
    la_hi, la_lo = split(la)
    inc = inc_ref[...]
    b = jnp.dot(inc, la_hi, preferred_element_type=F32) + jnp.dot(inc, la_lo, preferred_element_type=F32)
    gate_chunks(n_gate_chunks // 3, 2 * n_gate_chunks // 3)

    b3 = b.reshape(n_sub, 8, width)
    la3 = la.reshape(n_sub, 8, width)
    sub = lax.broadcasted_iota(jnp.int32, (1, 8, width), 1)

    def hi_step(y, bit):
        return jnp.where((sub & bit) == 0, pltpu.roll(y, 8 - bit, axis=1), y)

    def lo_step(y, bit):
        return jnp.where((sub & bit) != 0, pltpu.roll(y, bit, axis=1), y)

    def spread(y, groups, pick):
        y4 = y.reshape(n_sub // groups, groups, 8, width)
        return jnp.broadcast_to(y4[:, pick:pick + 1], y4.shape).reshape(n_sub, 8, width)

    hi = {1: b3}
    lo = {1: b3 - la3}
    for bit in (1, 2, 4):
        hi[2 * bit] = hi_step(hi[bit], bit)
        lo[2 * bit] = lo_step(lo[bit], bit)
    for groups in (2, 4, 8):
        hi[8 * groups] = spread(hi[8], groups, groups - 1)
        lo[8 * groups] = spread(lo[8], groups, 0)

    q_all = jnp.concatenate([q_ref[h] for h in range(GLA_HEADS)], axis=1).astype(F32) * (GLA_DK ** -0.5)
    k_all = jnp.concatenate([k_ref[h] for h in range(GLA_HEADS)], axis=1).astype(F32)
    q3 = q_all.reshape(n_sub, 8, width)
    k3 = k_all.reshape(n_sub, 8, width)

    def q_side(m):
        return (q3 * jnp.exp(b3 - lo[m])).reshape(t_rows, width).astype(BF16)

    def k_side(m):
        return (k3 * jnp.exp(hi[m] - b3)).reshape(t_rows, width).astype(BF16)

    qd = {m: q_side(m) for m in GLA_LEVELS + (GLA_CHUNK,)}
    kd = {m: (k_all.astype(BF16) if m == 1 else k_side(m)) for m in GLA_LEVELS + (GLA_CHUNK,)}
    diag = q_all * k_all
    v_heads = [jnp.concatenate([v_ref[2 * h], v_ref[2 * h + 1]], axis=1) for h in range(GLA_HEADS)]
    g_out = gout_ref[...]
    nt = (((1,), (1,)), ((), ()))
    chunks_per_atile = GLA_ATILE // GLA_CHUNK

    for a in range(t_rows // GLA_ATILE):
        if a == 1:
            gate_chunks(2 * n_gate_chunks // 3, n_gate_chunks)
        arows = slice(a * GLA_ATILE, (a + 1) * GLA_ATILE)
        for h in range(GLA_HEADS):
            cols = slice(h * GLA_DK_PAD, (h + 1) * GLA_DK_PAD)
            scores = None
            for li, m in enumerate(GLA_LEVELS):
                s = lax.dot_general(qd[m][arows, cols], kd[m][arows, cols], nt, preferred_element_type=F32)
                s = s * lvl_ref[li]
                scores = s if scores is None else scores + s
            v_a = v_heads[h][arows]
            rd = jnp.sum(diag[arows, cols], axis=-1, keepdims=True)
            intra = jnp.dot(scores.astype(BF16), v_a, preferred_element_type=F32) + rd * v_a.astype(F32)
            for cc in range(chunks_per_atile):
                c = a * chunks_per_atile + cc
                rows = slice(c * GLA_CHUNK, (c + 1) * GLA_CHUNK)
                st = st_ref[h]
                inter = lax.dot_general(qd[GLA_CHUNK][rows, cols], st.astype(BF16), nt,
                                        preferred_element_type=F32)
                o = intra[cc * GLA_CHUNK:(cc + 1) * GLA_CHUNK] + inter
                ms = jnp.sum(o * o, axis=-1, keepdims=True) * (1.0 / GLA_DV)
                o_n = (o * lax.rsqrt(ms + EPS) * g_out).astype(o_ref.dtype)
                o_ref[rows, h * GLA_DV:(h + 1) * GLA_DV] = o_n[:, :GLA_DV]
                last = c * GLA_CHUNK + GLA_CHUNK - 1
                decay = jnp.exp(b[last:last + 1, cols])
                upd = lax.dot_general(v_heads[h][rows], kd[GLA_CHUNK][rows, cols],
                                      (((0,), (0,)), ((), ())), preferred_element_type=F32)
                st_ref[h] = st * decay + upd


def _gla(glaqkv, gl, wup_hi, wup_lo, bg, gout, h, w_gates, l, batch, seq):
    m = gl.shape[0]
    d_model = h.shape[1]
    n_gates = N_GATE_COLS
    tile = GLA_TILE
    tiles_per_seq = seq // tile
    inc, lvl = _gla_static_matrices(tile)
    row = lambda b, t: b * tiles_per_seq + t
    width = GLA_HEADS * GLA_DK_PAD
    return pl.pallas_call(
        _gla_kernel,
        grid=(batch, tiles_per_seq),
        in_specs=[pl.BlockSpec((GLA_HEADS, tile, 128), lambda b, t: (0, row(b, t), 0)),
                  pl.BlockSpec((GLA_HEADS, tile, 128), lambda b, t: (1, row(b, t), 0)),
                  pl.BlockSpec((2 * GLA_HEADS, tile, 128), lambda b, t: (1, row(b, t), 0)),
                  pl.BlockSpec((tile, 128), lambda b, t: (row(b, t), 0)),
                  pl.BlockSpec((None, 128, width), lambda b, t: (l, 0, 0)),
                  pl.BlockSpec((None, 128, width), lambda b, t: (l, 0, 0)),
                  pl.BlockSpec((None, 1, width), lambda b, t: (l, 0, 0)),
                  pl.BlockSpec((None, 1, GLA_DV_PAD), lambda b, t: (l, 0, 0)),
                  pl.BlockSpec((tile, tile), lambda b, t: (0, 0)),
                  pl.BlockSpec((len(GLA_LEVELS), GLA_ATILE, GLA_ATILE), lambda b, t: (0, 0, 0)),
                  pl.BlockSpec((tile, d_model), lambda b, t: (row(b, t), 0)),
                  pl.BlockSpec((None, d_model, n_gates), lambda b, t: (l, 0, 0),
                               pipeline_mode=pl.Buffered(1))],
        out_specs=[pl.BlockSpec((tile, GLA_HEADS * GLA_DV), lambda b, t: (row(b, t), 0)),
                   pl.BlockSpec((tile, n_gates), lambda b, t: (row(b, t), 0))],
        out_shape=[jax.ShapeDtypeStruct((m, GLA_HEADS * GLA_DV), BF16),
                   jax.ShapeDtypeStruct((m, n_gates), BF16)],
        scratch_shapes=[pltpu.VMEM((GLA_HEADS, GLA_DV_PAD, GLA_DK_PAD), F32)],
        compiler_params=_cparams(2),
        name="gla_gates",
    )(glaqkv, glaqkv, glaqkv, gl, wup_hi, wup_lo, bg, gout, inc, lvl, h, w_gates)


def _merge_out_kernel(oa_ref, ob_ref, oc_ref, ga_ref, gb_ref, gc_ref, pa_ref, pb_ref, pc_ref,
                      wo_ref, x_ref, g_ref, o_ref, h_ref):
    for c in range(oa_ref.shape[1] // MERGE_CHUNK):
        rows = slice(c * MERGE_CHUNK, (c + 1) * MERGE_CHUNK)
        oa = jnp.concatenate([oa_ref[hh, rows, :] for hh in range(A_HPG)], axis=1)
        ya = jnp.dot(oa, pa_ref[...], preferred_element_type=F32)
        yb = jnp.dot(ob_ref[rows, :], pb_ref[...], preferred_element_type=F32)
        yc = jnp.dot(oc_ref[rows, :], pc_ref[...], preferred_element_type=F32)
        y = (ga_ref[rows, :].astype(F32) * ya + gb_ref[rows, :].astype(F32) * yb
             + gc_ref[rows, :].astype(F32) * yc).astype(BF16)
        x_new = x_ref[rows, :] + jnp.dot(y, wo_ref[...], preferred_element_type=F32)
        o_ref[rows, :] = x_new
        h_ref[rows, :] = _rms(x_new, g_ref[...]).astype(h_ref.dtype)


def _merge_out(oa, ob, oc, gates, pa, pb, pc, w_out, x, g, l, tm=512):
    m, d = x.shape
    const = lambda i: (l, 0, 0)
    resident = lambda arr: pl.BlockSpec((None,) + arr.shape[1:], const, pipeline_mode=pl.Buffered(1))
    return pl.pallas_call(
        _merge_out_kernel,
        grid=(m // tm,),
        in_specs=[pl.BlockSpec((A_HPG, tm, A_HEAD_DIM), lambda i: (0, i, 0)),
                  pl.BlockSpec((tm, ob.shape[1]), lambda i: (i, 0)),
                  pl.BlockSpec((tm, oc.shape[1]), lambda i: (i, 0)),
                  pl.BlockSpec((tm, d), lambda i: (i, 0)),
                  pl.BlockSpec((tm, d), lambda i: (i, 1)),
                  pl.BlockSpec((tm, d), lambda i: (i, 2)),
                  resident(pa), resident(pb), resident(pc), resident(w_out),
                  pl.BlockSpec((tm, d), lambda i: (i, 0)),
                  pl.BlockSpec((None, 1, d), const)],
        out_specs=[pl.BlockSpec((tm, d), lambda i: (i, 0)),
                   pl.BlockSpec((tm, d), lambda i: (i, 0))],
        out_shape=[jax.ShapeDtypeStruct((m, d), F32),
                   jax.ShapeDtypeStruct((m, d), BF16)],
        compiler_params=_cparams(1, VMEM_LIMIT_BIG_BYTES),
        name="merge_out",
    )(oa, ob, oc, gates, gates, gates, pa, pb, pc, w_out, x, g)


def _ffn_out_kernel(a_ref, w_ref, x_ref, o_ref):
    o_ref[...] = x_ref[...] + jnp.dot(a_ref[...], w_ref[...], preferred_element_type=F32)


def _ffn_out(a, w, x, l, tm=512):
    m, kdim = a.shape
    n = w.shape[2]
    return pl.pallas_call(
        _ffn_out_kernel,
        grid=(m // tm,),
        in_specs=[pl.BlockSpec((tm, kdim), lambda i: (i, 0)),
                  pl.BlockSpec((None, kdim, n), lambda i: (l, 0, 0), pipeline_mode=pl.Buffered(1)),
                  pl.BlockSpec((tm, n), lambda i: (i, 0))],
        out_specs=pl.BlockSpec((tm, n), lambda i: (i, 0)),
        out_shape=jax.ShapeDtypeStruct((m, n), F32),
        compiler_params=_cparams(1, VMEM_LIMIT_BIG_BYTES),
        name="ffn_out",
    )(a, w, x)


def _ffn_in_kernel(h_ref, wg_ref, wu_ref, o_ref, wg_scr, wu_scr):
    wg_scr[...] = wg_ref[...].astype(wg_scr.dtype)
    wu_scr[...] = wu_ref[...].astype(wu_scr.dtype)
    for c in range(h_ref.shape[0] // FFN_CHUNK):
        rows = slice(c * FFN_CHUNK, (c + 1) * FFN_CHUNK)
        h = h_ref[rows, :]
        g = jnp.dot(h, wg_scr[...], preferred_element_type=F32)
        u = jnp.dot(h, wu_scr[...], preferred_element_type=F32)
        o_ref[rows, :] = (g * jax.nn.sigmoid(g) * u).astype(o_ref.dtype)


def _ffn_in(h, w, l, tm=2048, tn=512):
    m, d = h.shape
    f = w.shape[2] // 2
    up_off = f // tn
    return pl.pallas_call(
        _ffn_in_kernel,
        grid=(m // tm, f // tn),
        in_specs=[pl.BlockSpec((tm, d), lambda i, j: (i, 0)),
                  pl.BlockSpec((None, d, tn), lambda i, j: (l, 0, j)),
                  pl.BlockSpec((None, d, tn), lambda i, j: (l, 0, up_off + j))],
        out_specs=pl.BlockSpec((tm, tn), lambda i, j: (i, j)),
        out_shape=jax.ShapeDtypeStruct((m, f), BF16),
        scratch_shapes=[pltpu.VMEM((d, tn), BF16), pltpu.VMEM((d, tn), BF16)],
        compiler_params=_cparams(2),
        name="ffn_in",
    )(h, w, w)


def _pad_heads(w, off, n_heads, width, padded):
    lead = w.shape[:-1]
    cols = w[..., off:off + n_heads * width].reshape(lead + (n_heads, width))
    cols = jnp.pad(cols, [(0, 0)] * (len(lead) + 1) + [(0, padded - width)])
    return cols.reshape(lead + (n_heads * padded,))


def _prepare_params(w_in, gla_gate_up, gla_gate_b, gla_out_g, sgu_w, sgu_b, w_branch):
    depth = w_in.shape[0]
    w_in_b = w_in.astype(BF16)
    w_attn = w_in_b[:, :, _OFF_AQ:_OFF_BQ]
    n_gla = _OFF_CU - _OFF_BQ
    w_gla = jnp.pad(w_in_b[:, :, _OFF_BQ:_OFF_CU], ((0, 0), (0, 0), (0, W_GLA_COLS - n_gla)))
    w_sgu = w_in_b[:, :, _OFF_CU:_OFF_GATES]
    w_gates = w_in_b[:, :, _OFF_GATES:]
    wup = _pad_heads(gla_gate_up, 0, GLA_HEADS, GLA_DK, GLA_DK_PAD)
    wup = jnp.pad(wup, ((0, 0), (0, 128 - GLA_RANK), (0, 0)))
    wup_hi = wup.astype(BF16)
    wup_lo = (wup - wup_hi.astype(F32)).astype(BF16)
    bg = _pad_heads(gla_gate_b.reshape(depth, 1, -1), 0, GLA_HEADS, GLA_DK, GLA_DK_PAD)
    gout = jnp.pad(gla_out_g.reshape(depth, 1, GLA_DV), ((0, 0), (0, 0), (0, GLA_DV_PAD - GLA_DV)))
    tril = np.tril(np.ones((SGU_CHUNK, SGU_CHUNK), dtype=bool))
    ws = jnp.where(tril, sgu_w, 0.0).astype(BF16)
    bs_tile = jnp.repeat(jnp.swapaxes(sgu_b, 1, 2), SGU_GW, axis=2)
    pa = w_branch[:, :A_WIDTH].astype(BF16)
    pb = w_branch[:, A_WIDTH:A_WIDTH + GLA_HEADS * GLA_DV].astype(BF16)
    pc = w_branch[:, A_WIDTH + GLA_HEADS * GLA_DV:].astype(BF16)
    return dict(w_attn=w_attn, w_gla=w_gla, w_sgu=w_sgu, w_gates=w_gates, wup_hi=wup_hi, wup_lo=wup_lo,
                bg=bg, gout=gout, ws=ws, bs_tile=bs_tile, pa=pa, pb=pb, pc=pc)


def kernel(x, rel_bias, norm1_g, w_in, q_norm_g, k_norm_g, gla_gate_up, gla_gate_b, gla_out_g,
           sgu_ln_g, sgu_ln_b, sgu_w, sgu_b, w_branch, w_out, norm2_g, w_ffn_in, w_ffn_out):
    batch, seq, d = x.shape
    depth = w_in.shape[0]
    m = batch * seq
    assert seq % (A_BLK * max(DILATIONS)) == 0 and seq % GLA_TILE == 0 and d == D_MODEL
    p = _prepare_params(w_in, gla_gate_up, gla_gate_b, gla_out_g, sgu_w, sgu_b, w_branch)
    w_out_b = w_out.astype(BF16)
    w_ffn_out_b = w_ffn_out.astype(BF16)
    n1 = norm1_g.reshape(depth, 1, d)
    n2 = norm2_g.reshape(depth, 1, d)
    gq = q_norm_g.reshape(depth, 1, A_HEAD_DIM)
    gk = k_norm_g.reshape(depth, 1, A_HEAD_DIM)
    ln_g = sgu_ln_g.reshape(depth, 1, SGU_WIDTH)
    ln_b = sgu_ln_b.reshape(depth, 1, SGU_WIDTH)
    bias = _attn_bias_tiles(rel_bias)
    xf = x.reshape(m, d)
    for l in range(depth):
        qkv0, h = _attn_proj_norm(xf, n1, p["w_attn"], gq, gk, l, batch, seq)
        qkvs = [qkv0] + [_attn_proj(h, p["w_attn"], gq, gk, l, gi, batch, seq) for gi in (1, 2)]
        glaqkv, gl = _gla_proj(h, p["w_gla"], l)
        o_a, o_c = _attn_sgu(qkvs, bias, h, p["w_sgu"], ln_g, ln_b, p["ws"], p["bs_tile"], l, batch, seq)
        o_b, gates = _gla(glaqkv, gl, p["wup_hi"], p["wup_lo"], p["bg"], p["gout"], h, p["w_gates"],
                          l, batch, seq)
        xf, h2 = _merge_out(o_a, o_b, o_c, gates, p["pa"], p["pb"], p["pc"], w_out_b, xf, n2, l)
        a = _ffn_in(h2, w_ffn_in, l)
        xf = _ffn_out(a, w_ffn_out_b, xf, l)
    return xf.reshape(batch, seq, d)
```
